```python
import math
import jax, jax.numpy as jnp
from jax import lax
import numpy as np

D_MODEL = 1024
BATCH = 8
SEQ = 8192
DEPTH = 1
DEC_BATCH = 128
DEC_SEQ = 8
PAST_LEN = 8192
PAGE_SIZE = 128

F32 = jnp.float32
EPS = 1e-6
D_MIX = D_MODEL
H_A = 4
DV_A = D_MIX // (2 * H_A)
DK_A = DV_A // 2
ROT_DIM = DK_A // 4
ROPE_THETA = 500000.0
Q_BLOCK = 128
H_B = 4
DK_B = (D_MIX - H_A * DV_A) // H_B
DV_B = DK_B
HGRN_CHUNK = 64
MEM_LEN = 256
H_M = 4
DH_M = D_MODEL // H_M
N_KEYS = 128
N_EXPERTS = N_KEYS * N_KEYS
PEER_HEADS = 8
PEER_TOPK = 16
PEER_QDIM = 256
PEER_HALF = PEER_QDIM // 2
PEER_BLOCK = 256
A_QK = H_A * 2 * DK_A
A_V = H_A * DV_A
B_K = H_B * DK_B
B_V = H_B * DV_B
IN_COLS = 2 * A_QK + A_V + 2 * B_K + 2 * B_V
SPLITS = (A_QK, 2 * A_QK, 2 * A_QK + A_V, 2 * A_QK + A_V + B_K, 2 * A_QK + A_V + 2 * B_K, 2 * A_QK + A_V + 2 * B_K + B_V)

kernel_name = "hymba_diffattn_hgrn2_peer_decoder_step"


def rmsnorm(x, w):
    xf = x.astype(F32)
    y = xf * lax.rsqrt(jnp.mean(xf * xf, axis=-1, keepdims=True) + EPS)
    return (y * w.astype(F32)).astype(x.dtype)


def rope(x, pos):
    half = ROT_DIM // 2
    inv = ROPE_THETA ** (-jnp.arange(half, dtype=F32) * 2.0 / ROT_DIM)
    ang = pos.astype(F32)[:, None] * inv[None, :]
    cos = jnp.cos(ang)[None, :, None, None, :].astype(x.dtype)
    sin = jnp.sin(ang)[None, :, None, None, :].astype(x.dtype)
    x1 = x[..., :half]
    x2 = x[..., half:ROT_DIM]
    return jnp.concatenate([x1 * cos - x2 * sin, x2 * cos + x1 * sin, x[..., ROT_DIM:]], axis=-1)


def mixer_inputs(h, w_in_l, lower_bounds, layer, pos):
    B, L, _ = h.shape
    p = h @ w_in_l
    q_a, k_a, v_a, q_b, f_b, i_b, g_b = jnp.split(p, SPLITS, axis=-1)
    q_a = rope(q_a.reshape(B, L, H_A, 2, DK_A), pos)
    k_a = rope(k_a.reshape(B, L, H_A, 2, DK_A), pos)
    v_a = v_a.reshape(B, L, H_A, DV_A)
    lb = jnp.cumsum(jax.nn.softmax(lower_bounds.astype(F32), axis=0), axis=0)[layer].reshape(H_B, DK_B)
    fgate = lb + (1.0 - lb) * jax.nn.sigmoid(f_b.reshape(B, L, H_B, DK_B).astype(F32))
    logf = jnp.log(fgate)
    k_b = 1.0 - fgate
    q_b = q_b.reshape(B, L, H_B, DK_B)
    i_b = i_b.reshape(B, L, H_B, DV_B)
    g_b = g_b.reshape(B, L, H_B, DV_B)
    return q_a, k_a, v_a, q_b, k_b, i_b, logf, g_b


def diff_mix(s, lam):
    p = jax.nn.softmax(s, axis=-1)
    return p[:, :, 0] - lam * p[:, :, 1]


def diff_attend_block(q, k, v, q_pos, k_pos, lam):
    s = jnp.einsum('bqhmd,bkhmd->bhmqk', q, k).astype(F32) * DK_A ** -0.5
    s = jnp.where(k_pos[None, :] <= q_pos[:, None], s, -jnp.inf)
    a = diff_mix(s, lam)
    return jnp.einsum('bhqk,bkhd->bqhd', a.astype(v.dtype), v)


def prompt_diff_attention(q, k, v, lam):
    B, L = q.shape[:2]
    nb = L // Q_BLOCK
    qb = jnp.moveaxis(q.reshape(B, nb, Q_BLOCK, H_A, 2, DK_A), 1, 0)
    pos = jnp.arange(L)
    qpos = pos.reshape(nb, Q_BLOCK)
    out = lax.map(lambda a: diff_attend_block(a[0], k, v, a[1], pos, lam), (qb, qpos))
    return jnp.moveaxis(out, 0, 1).reshape(B, L, H_A, DV_A)


def sample_diff_attention(q, k_new, v_new, k_past, v_past, lam):
    n_past = k_past.shape[1]
    ln = q.shape[1]
    s_past = jnp.einsum('bqhmd,bkhmd->bhmqk', q, k_past).astype(F32)
    s_new = jnp.einsum('bqhmd,bkhmd->bhmqk', q, k_new).astype(F32)
    s_new = jnp.where(jnp.tril(jnp.ones((ln, ln), bool)), s_new, -jnp.inf)
    s = jnp.concatenate([s_past, s_new], axis=-1) * DK_A ** -0.5
    a = diff_mix(s, lam).astype(v_new.dtype)
    return (jnp.einsum('bhqk,bkhd->bqhd', a[..., :n_past], v_past)
            + jnp.einsum('bhqk,bkhd->bqhd', a[..., n_past:], v_new))


def hgrn2_chunked(q, k, v, logf, s0):
    B, L, H, DK = q.shape
    DV = v.shape[-1]
    chunk = HGRN_CHUNK if L % HGRN_CHUNK == 0 else L
    n = L // chunk

    def to_chunks(t):
        return jnp.moveaxis(t.astype(F32).reshape(B, n, chunk, H, t.shape[-1]), 1, 0)

    causal = jnp.tril(jnp.ones((chunk, chunk), bool))[None, :, :, None, None]

    def step(S, inp):
        qc, kc, vc, lc = inp
        b = jnp.cumsum(lc, axis=1)
        o_inter = jnp.einsum('bthd,bhde->bthe', qc * jnp.exp(b), S)
        diff = b[:, :, None] - b[:, None, :]
        decay = jnp.exp(jnp.where(causal, diff, -jnp.inf))
        att = jnp.einsum('bthd,bshd,btshd->bhts', qc, kc, decay)
        o_intra = jnp.einsum('bhts,bshe->bthe', att, vc)
        b_last = b[:, -1]
        k_dec = kc * jnp.exp(b_last[:, None] - b)
        S_new = jnp.exp(b_last)[..., None] * S + jnp.einsum('bshd,bshe->bhde', k_dec, vc)
        return S_new, o_inter + o_intra

    S_fin, o = lax.scan(step, s0.astype(F32), (to_chunks(q), to_chunks(k), to_chunks(v), to_chunks(logf)))
    o = jnp.moveaxis(o, 0, 1).reshape(B, L, H, DV)
    return o, S_fin


def mix_output(oa, ob, gb, lam_init, diff_ln_w_l, hgrn_norm_w_l, w_out_l):
    B, L = oa.shape[:2]
    oa = rmsnorm(oa, diff_ln_w_l) * (1.0 - lam_init)
    ob = rmsnorm(ob.astype(gb.dtype), hgrn_norm_w_l) * jax.nn.silu(gb)
    o = jnp.concatenate([oa.reshape(B, L, A_V), ob.reshape(B, L, B_V)], axis=-1)
    return o @ w_out_l


def mem_kv(mem, norm_w, w_mk_l, w_mv_l):
    B = mem.shape[0]
    m = rmsnorm(mem, norm_w)
    return ((m @ w_mk_l).reshape(B, MEM_LEN, H_M, DH_M), (m @ w_mv_l).reshape(B, MEM_LEN, H_M, DH_M))


def mem_attend(h, mk, mv, w_mq_l, w_mo_l):
    B, L, _ = h.shape
    q = (h @ w_mq_l).reshape(B, L, H_M, DH_M)
    s = jnp.einsum('blhd,bmhd->bhlm', q, mk).astype(F32) * DH_M ** -0.5
    p = jax.nn.softmax(s, axis=-1)
    o = jnp.einsum('bhlm,bmhd->blhd', p.astype(mv.dtype), mv).reshape(B, L, D_MODEL)
    return o @ w_mo_l


def peer_ffn(h, wq, keys, u, v):
    shp = h.shape
    t = h.reshape(-1, D_MODEL)
    n = t.shape[0]
    n_pad = -(-n // PEER_BLOCK) * PEER_BLOCK
    t = jnp.pad(t, ((0, n_pad - n), (0, 0)))
    blocks = t.reshape(n_pad // PEER_BLOCK, PEER_BLOCK, D_MODEL)

    def one(xb):
        q = (xb @ wq).reshape(-1, PEER_HEADS, 2, PEER_HALF)
        s = jnp.einsum('thpd,hpkd->thpk', q, keys).astype(F32)
        s1, i1 = lax.top_k(s[:, :, 0], PEER_TOPK)
        s2, i2 = lax.top_k(s[:, :, 1], PEER_TOPK)
        nt = s1.shape[0]
        cand = (s1[..., :, None] + s2[..., None, :]).reshape(nt, PEER_HEADS, PEER_TOPK * PEER_TOPK)
        cidx = (i1[..., :, None] * N_KEYS + i2[..., None, :]).reshape(nt, PEER_HEADS, PEER_TOPK * PEER_TOPK)
        sc, pos = lax.top_k(cand, PEER_TOPK)
        idx = jnp.take_along_axis(cidx, pos, axis=-1)
        g = jax.nn.softmax(sc, axis=-1)
        act = jax.nn.gelu(jnp.einsum('td,thkd->thk', xb, u[idx]).astype(F32), approximate=False)
        return jnp.einsum('thk,thkd->td', (g * act).astype(xb.dtype), v[idx])

    y = lax.map(one, blocks).reshape(n_pad, D_MODEL)[:n]
    return y.reshape(shp)


def setup_inputs(seed: int = 0) -> dict:
    key = jax.random.key(seed)
    ks = jax.random.split(key, 32)
    n_pages = PAST_LEN // PAGE_SIZE
    n_used = DEC_BATCH * n_pages
    n_pool = n_used + n_used // 4 + 1
    nrm = lambda k, shape, scale: jax.random.normal(k, shape, F32) * scale
    gain = lambda k, shape: 1.0 + 0.02 * jax.random.normal(k, shape, F32)
    page_table = jax.random.permutation(ks[0], n_pool)[:n_used].reshape(DEC_BATCH, n_pages).astype(jnp.int32)
    return {
        "x_prompt": nrm(ks[1], (BATCH, SEQ, D_MODEL), 1.0),
        "x_sample": nrm(ks[2], (DEC_BATCH, DEC_SEQ, D_MODEL), 1.0),
        "cache_attn_k": nrm(ks[3], (DEPTH, n_pool, PAGE_SIZE, H_A, 2, DK_A), 1.0),
        "cache_attn_v": nrm(ks[4], (DEPTH, n_pool, PAGE_SIZE, H_A, DV_A), 1.0),
        "cache_mem_k": nrm(ks[5], (DEPTH, DEC_BATCH, MEM_LEN, H_M, DH_M), 1.0),
        "cache_mem_v": nrm(ks[6], (DEPTH, DEC_BATCH, MEM_LEN, H_M, DH_M), 1.0),
        "state_hgrn": nrm(ks[7], (DEPTH, DEC_BATCH, H_B, DK_B, DV_B), 0.5),
        "page_table": page_table,
        "mem_prompt": nrm(ks[8], (BATCH, MEM_LEN, D_MODEL), 1.0),
        "norm_mix_w": gain(ks[9], (DEPTH, D_MODEL)),
        "w_in": nrm(ks[10], (DEPTH, D_MODEL, IN_COLS), D_MODEL ** -0.5),
        "lambda_q1": nrm(ks[11], (DEPTH, DK_A), 0.1),
        "lambda_k1": nrm(ks[12], (DEPTH, DK_A), 0.1),
        "lambda_q2": nrm(ks[13], (DEPTH, DK_A), 0.1),
        "lambda_k2": nrm(ks[14], (DEPTH, DK_A), 0.1),
        "diff_ln_w": gain(ks[15], (DEPTH, DV_A)),
        "hgrn_lower_bounds": nrm(ks[16], (DEPTH + 1, B_K), 0.5),
        "hgrn_norm_w": gain(ks[17], (DEPTH, DV_B)),
        "w_out": nrm(ks[18], (DEPTH, D_MIX, D_MODEL), D_MIX ** -0.5),
        "norm_mem_q_w": gain(ks[19], (DEPTH, D_MODEL)),
        "norm_mem_kv_w": gain(ks[20], (DEPTH, D_MODEL)),
        "w_mq": nrm(ks[21], (DEPTH, D_MODEL, D_MODEL), D_MODEL ** -0.5),
        "w_mk": nrm(ks[22], (DEPTH, D_MODEL, D_MODEL), D_MODEL ** -0.5),
        "w_mv": nrm(ks[23], (DEPTH, D_MODEL, D_MODEL), D_MODEL ** -0.5),
        "w_mo": nrm(ks[24], (DEPTH, D_MODEL, D_MODEL), D_MODEL ** -0.5),
        "norm_ffn_w": gain(ks[25], (DEPTH, D_MODEL)),
        "peer_wq": nrm(ks[26], (DEPTH, D_MODEL, PEER_HEADS * PEER_QDIM), D_MODEL ** -0.5),
        "peer_keys": nrm(ks[27], (DEPTH, PEER_HEADS, 2, N_KEYS, PEER_HALF), PEER_HALF ** -0.5),
        "peer_u": nrm(ks[28], (DEPTH, N_EXPERTS, D_MODEL), D_MODEL ** -0.5),
        "peer_v": nrm(ks[29], (DEPTH, N_EXPERTS, D_MODEL), PEER_HEADS ** -0.5),
        "final_norm_w": gain(ks[30], (D_MODEL,)),
    }


def reference(x_prompt, x_sample, cache_attn_k, cache_attn_v, cache_mem_k, cache_mem_v, state_hgrn, page_table, mem_prompt,
              norm_mix_w, w_in, lambda_q1, lambda_k1, lambda_q2, lambda_k2, diff_ln_w, hgrn_lower_bounds, hgrn_norm_w, w_out,
              norm_mem_q_w, norm_mem_kv_w, w_mq, w_mk, w_mv, w_mo, norm_ffn_w, peer_wq, peer_keys, peer_u, peer_v, final_norm_w):
    B, L = x_prompt.shape[:2]
    DB, LS = x_sample.shape[:2]
    past = page_table.shape[1] * PAGE_SIZE
    pos_p = jnp.arange(L)
    pos_s = past + jnp.arange(LS)
    xp, xs = x_prompt, x_sample
    kp_l, vp_l, sp_l, mkp_l, mvp_l, ks_l, vs_l, ss_l = [], [], [], [], [], [], [], []
    for l in range(DEPTH):
        lam_init = 0.8 - 0.6 * math.exp(-0.3 * l)
        lam = (jnp.exp(jnp.sum(lambda_q1[l].astype(F32) * lambda_k1[l].astype(F32)))
               - jnp.exp(jnp.sum(lambda_q2[l].astype(F32) * lambda_k2[l].astype(F32))) + lam_init)
        h = rmsnorm(xp, norm_mix_w[l])
        qa, ka, va, qb, kb, vb, logf, gb = mixer_inputs(h, w_in[l], hgrn_lower_bounds, l, pos_p)
        oa = prompt_diff_attention(qa, ka, va, lam)
        ob, sp = hgrn2_chunked(qb, kb, vb, logf, jnp.zeros((B, H_B, DK_B, DV_B), F32))
        xp = xp + mix_output(oa, ob, gb, lam_init, diff_ln_w[l], hgrn_norm_w[l], w_out[l])
        mk, mv = mem_kv(mem_prompt, norm_mem_kv_w[l], w_mk[l], w_mv[l])
        xp = xp + mem_attend(rmsnorm(xp, norm_mem_q_w[l]), mk, mv, w_mq[l], w_mo[l])
        xp = xp + peer_ffn(rmsnorm(xp, norm_ffn_w[l]), peer_wq[l], peer_keys[l], peer_u[l], peer_v[l])
        kp_l.append(ka); vp_l.append(va); sp_l.append(sp); mkp_l.append(mk); mvp_l.append(mv)
        h = rmsnorm(xs, norm_mix_w[l])
        qa, ka, va, qb, kb, vb, logf, gb = mixer_inputs(h, w_in[l], hgrn_lower_bounds, l, pos_s)
        k_past = cache_attn_k[l, page_table].reshape(DB, past, H_A, 2, DK_A)
        v_past = cache_attn_v[l, page_table].reshape(DB, past, H_A, DV_A)
        oa = sample_diff_attention(qa, ka, va, k_past, v_past, lam)
        ob, ss = hgrn2_chunked(qb, kb, vb, logf, state_hgrn[l])
        xs = xs + mix_output(oa, ob, gb, lam_init, diff_ln_w[l], hgrn_norm_w[l], w_out[l])
        xs = xs + mem_attend(rmsnorm(xs, norm_mem_q_w[l]), cache_mem_k[l], cache_mem_v[l], w_mq[l], w_mo[l])
        xs = xs + peer_ffn(rmsnorm(xs, norm_ffn_w[l]), peer_wq[l], peer_keys[l], peer_u[l], peer_v[l])
        ks_l.append(ka); vs_l.append(va); ss_l.append(ss)
    y_prompt = rmsnorm(xp, final_norm_w)
    y_sample = rmsnorm(xs, final_norm_w)
    new_attn_k_prompt = jnp.stack(kp_l)
    new_attn_v_prompt = jnp.stack(vp_l)
    new_state_hgrn_prompt = jnp.stack(sp_l)
    new_mem_k_prompt = jnp.stack(mkp_l)
    new_mem_v_prompt = jnp.stack(mvp_l)
    new_attn_k_sample = jnp.stack(ks_l)
    new_attn_v_sample = jnp.stack(vs_l)
    new_state_hgrn_sample = jnp.stack(ss_l)
    return (y_prompt, y_sample, new_attn_k_prompt, new_attn_v_prompt, new_state_hgrn_prompt, new_mem_k_prompt, new_mem_v_prompt, new_attn_k_sample, new_attn_v_sample, new_state_hgrn_sample)
```

```python
import functools
import math

import jax
import jax.numpy as jnp
from jax import lax
from jax.experimental import pallas as pl
from jax.experimental.pallas import tpu as pltpu

F32 = jnp.float32
BF16 = jnp.bfloat16
EPS = 1e-6
NEG_INF = float("-inf")

D_MODEL = 1024
PAGE_SIZE = 128
H_A = 4
DV_A = 128
DK_A = 64
ROT_DIM = 16
ROPE_THETA = 500000.0
H_B = 4
DK_B = 128
DV_B = 128
HGRN_CHUNK = 64
HGRN_SUB = 16
MEM_LEN = 256
H_M = 4
DH_M = 256
N_KEYS = 128
PEER_HEADS = 8
PEER_TOPK = 16
PEER_HALF = 128
SEG = 512
N_SEG = 7

VMEM_LIMIT = 56 * 1024 * 1024


def _cparams(*sem):
    return pltpu.CompilerParams(dimension_semantics=sem, vmem_limit_bytes=VMEM_LIMIT)


def _rms(x, w):
    return x * lax.rsqrt(jnp.mean(x * x, axis=-1, keepdims=True) + EPS) * w


def _dot_nt(a, b):
    return lax.dot_general(a, b, (((1,), (1,)), ((), ())), preferred_element_type=F32)


def _dot(a, b):
    return jnp.dot(a, b, preferred_element_type=F32)


def _inproj_body(x_ref, nw_ref, w_ref, lbp_ref, rc_ref, rs1_ref, rs2_ref,
                 ka_ref, va_ref, qab_ref, kab_ref, vab_ref,
                 qb_ref, kb_ref, ib_ref, lf_ref, gb_ref, *, layer):
    hb = _rms(x_ref[...], nw_ref[...]).astype(BF16)

    def seg(i):
        return _dot(hb, w_ref[:, i * SEG:(i + 1) * SEG])

    rc, rs1, rs2 = rc_ref[...], rs1_ref[...], rs2_ref[...]

    def rope_block(blk):
        return blk * rc + pltpu.roll(blk, 128 - ROT_DIM // 2, 1) * rs1 + pltpu.roll(blk, ROT_DIM // 2, 1) * rs2

    qa = seg(0)
    ka = seg(1)
    for g in range(SEG // 128):
        sl = slice(g * 128, (g + 1) * 128)
        qab_ref[:, sl] = (rope_block(qa[:, sl]) * (DK_A ** -0.5)).astype(BF16)
        kr = rope_block(ka[:, sl])
        ka_ref[:, sl] = kr
        kab_ref[:, sl] = kr.astype(BF16)
    va = seg(2)
    va_ref[...] = va
    vab_ref[...] = va.astype(BF16)
    qb_ref[...] = seg(3)
    lbp = lbp_ref[...]
    e = jnp.exp(lbp - jnp.max(lbp, axis=0, keepdims=True))
    lb = jnp.sum(e[:layer + 1], axis=0, keepdims=True) / jnp.sum(e, axis=0, keepdims=True)
    fg = lb + (1.0 - lb) * jax.nn.sigmoid(seg(4))
    lf_ref[...] = jnp.log(fg)
    kb_ref[...] = 1.0 - fg
    ib_ref[...] = seg(5).astype(BF16)
    gb_ref[...] = seg(6)


def _rope_tables(pos):
    half = ROT_DIM // 2
    inv = ROPE_THETA ** (-jnp.arange(half, dtype=F32) * 2.0 / ROT_DIM)
    ang = pos.astype(F32)[:, None] * inv[None, :]
    cos, sin = jnp.cos(ang), jnp.sin(ang)
    p = pos.shape[0]
    ones = jnp.ones((p, DK_A - ROT_DIM), F32)
    zeros = jnp.zeros((p, DK_A - ROT_DIM), F32)
    zh = jnp.zeros((p, half), F32)
    rc = jnp.concatenate([cos, cos, ones], axis=1)
    rs1 = jnp.concatenate([-sin, zh, zeros], axis=1)
    rs2 = jnp.concatenate([zh, sin, zeros], axis=1)
    return tuple(jnp.concatenate([t, t], axis=1) for t in (rc, rs1, rs2))


def _inproj(x2d, norm_w, w_in_bf, lower_bounds, pos, layer, tm):
    n = x2d.shape[0]
    p = pos.shape[0]
    assert n % tm == 0 and p % tm == 0
    npb = p // tm
    rc, rs1, rs2 = _rope_tables(pos)
    row = lambda i: (i, 0)
    const = lambda i: (0, 0)
    tab = lambda i: (i % npb, 0)
    f32o = jax.ShapeDtypeStruct((n, SEG), F32)
    bf16o = jax.ShapeDtypeStruct((n, SEG), BF16)
    out_spec = pl.BlockSpec((tm, SEG), row)
    return pl.pallas_call(
        functools.partial(_inproj_body, layer=layer),
        grid=(n // tm,),
        in_specs=[
            pl.BlockSpec((tm, D_MODEL), row),
            pl.BlockSpec((1, D_MODEL), const),
            pl.BlockSpec((D_MODEL, N_SEG * SEG), const),
            pl.BlockSpec(lower_bounds.shape, const),
            pl.BlockSpec((tm, 128), tab),
            pl.BlockSpec((tm, 128), tab),
            pl.BlockSpec((tm, 128), tab),
        ],
        out_specs=[out_spec] * 10,
        out_shape=[f32o, f32o, bf16o, bf16o, bf16o, f32o, f32o, bf16o, f32o, f32o],
        compiler_params=_cparams("parallel"),
        name="inproj",
    )(x2d, norm_w.reshape(1, D_MODEL), w_in_bf, lower_bounds, rc, rs1, rs2)


def _split_maps(q):
    lane = lax.broadcasted_iota(jnp.int32, q.shape, 1)
    zero = jnp.zeros_like(q)
    return jnp.concatenate([jnp.where(lane < DK_A, q, zero), jnp.where(lane >= DK_A, q, zero)], axis=0)


def _online_update(s, v, m_ref, l_ref, acc_ref):
    m_prev = m_ref[...]
    m_new = jnp.maximum(m_prev, jnp.max(s, axis=-1, keepdims=True))
    alpha = jnp.exp(m_prev - m_new)
    p = jnp.exp(s - m_new)
    l_ref[...] = alpha * l_ref[...] + jnp.sum(p, axis=-1, keepdims=True)
    acc_ref[...] = alpha * acc_ref[...] + _dot(p.astype(BF16), v)
    m_ref[...] = m_new


def _diff_finish(acc, l, t, lam, lnw, out_scale):
    o = acc[:t] / l[:t] - lam * (acc[t:] / l[t:])
    return _rms(o, lnw) * out_scale


def _pattn_body(lam_ref, q_ref, k_ref, v_ref, lnw_ref, o_ref, m_ref, l_ref, acc_ref, *, tq, tk, out_scale):
    i = pl.program_id(2)
    q2 = _split_maps(q_ref[...])
    m_ref[...] = jnp.full(m_ref.shape, NEG_INF, F32)
    l_ref[...] = jnp.zeros(l_ref.shape, F32)
    acc_ref[...] = jnp.zeros(acc_ref.shape, F32)

    def block(j, masked):
        off = pl.multiple_of(j * tk, tk)
        s = _dot_nt(q2, k_ref[pl.ds(off, tk), :])
        if masked:
            r = lax.broadcasted_iota(jnp.int32, s.shape, 0)
            r = jnp.where(r >= tq, r - tq, r) + i * tq
            c = lax.broadcasted_iota(jnp.int32, s.shape, 1) + j * tk
            s = jnp.where(c <= r, s, NEG_INF)
        _online_update(s, v_ref[pl.ds(off, tk), :], m_ref, l_ref, acc_ref)

    n_full = i * (tq // tk)

    def full_block(j, carry):
        block(j, False)
        return carry

    lax.fori_loop(0, n_full, full_block, 0)
    for d in range(tq // tk):
        block(n_full + d, True)
    o_ref[...] = _diff_finish(acc_ref[...], l_ref[...], tq, lam_ref[...], lnw_ref[...], out_scale)


def _prompt_attention(lam, qab, kab, vab, diff_ln_w, out_scale, b, l, tq, tk):
    q3, k3, v3 = (t.reshape(b, l, SEG) for t in (qab, kab, vab))
    assert l % tq == 0 and tq % tk == 0
    out = pl.pallas_call(
        functools.partial(_pattn_body, tq=tq, tk=tk, out_scale=out_scale),
        grid=(b, H_A, l // tq),
        in_specs=[
            pl.BlockSpec((1, 1), lambda bi, h, i: (0, 0)),
            pl.BlockSpec((None, tq, 128), lambda bi, h, i: (bi, i, h)),
            pl.BlockSpec((None, l, 128), lambda bi, h, i: (bi, 0, h)),
            pl.BlockSpec((None, l, 128), lambda bi, h, i: (bi, 0, h)),
            pl.BlockSpec((1, DV_A), lambda bi, h, i: (0, 0)),
        ],
        out_specs=pl.BlockSpec((None, tq, 128), lambda bi, h, i: (bi, i, h)),
        out_shape=jax.ShapeDtypeStruct((b, l, SEG), F32),
        scratch_shapes=[pltpu.VMEM((2 * tq, 1), F32), pltpu.VMEM((2 * tq, 1), F32), pltpu.VMEM((2 * tq, 128), F32)],
        compiler_params=_cparams("parallel", "parallel", "arbitrary"),
        name="prompt_attn",
    )(lam, q3, k3, v3, diff_ln_w.reshape(1, DV_A))
    return out.reshape(b * l, SEG)


def _sattn_body(pt_ref, lam_ref, q_ref, *refs, n_grp, ls, out_scale):
    k_refs = refs[:n_grp]
    v_refs = refs[n_grp:2 * n_grp]
    kn_ref, vn_ref, lnw_ref, o_ref, q2_ref, m_ref, l_ref, acc_ref = refs[2 * n_grp:]
    j = pl.program_id(1)
    rows = 2 * ls

    @pl.when(j == 0)
    def _():
        q = q_ref[...]
        for h in range(H_A):
            q2_ref[h * rows:(h + 1) * rows, :] = _split_maps(q[:, h * 128:(h + 1) * 128])
        m_ref[...] = jnp.full(m_ref.shape, NEG_INF, F32)
        l_ref[...] = jnp.zeros(l_ref.shape, F32)
        acc_ref[...] = jnp.zeros(acc_ref.shape, F32)

    def head_update(h, k, v, mask):
        hs = slice(h * rows, (h + 1) * rows)
        s = _dot_nt(q2_ref[hs, :], k)
        if mask is not None:
            s = jnp.where(mask, s, NEG_INF)
        _online_update(s, v, m_ref.at[hs, :], l_ref.at[hs, :], acc_ref.at[hs, :])

    kp = [r[...].astype(BF16) for r in k_refs]
    vp = [r[...].astype(BF16) for r in v_refs]
    for h in range(H_A):
        hl = slice(h * 128, (h + 1) * 128)
        head_update(h, jnp.concatenate([t[:, hl] for t in kp], axis=0),
                    jnp.concatenate([t[:, hl] for t in vp], axis=0), None)

    @pl.when(j == pl.num_programs(1) - 1)
    def _():
        kn = kn_ref[...]
        vn = vn_ref[...]
        r = lax.broadcasted_iota(jnp.int32, (rows, kn.shape[0]), 0)
        r = jnp.where(r >= ls, r - ls, r)
        c = lax.broadcasted_iota(jnp.int32, (rows, kn.shape[0]), 1)
        mask = c <= r
        for h in range(H_A):
            hl = slice(h * 128, (h + 1) * 128)
            hs = slice(h * rows, (h + 1) * rows)
            head_update(h, kn[:, hl], vn[:, hl], mask)
            o_ref[:, hl] = _diff_finish(acc_ref[hs, :], l_ref[hs, :], ls, lam_ref[...], lnw_ref[...], out_scale)


def _sample_attention(lam, qab, kab, vab, cache_k, cache_v, page_table, diff_ln_w, out_scale, db, ls, n_grp):
    n_pages = page_table.shape[1]
    assert n_pages % n_grp == 0
    n_pool = cache_k.shape[0]
    ck = cache_k.reshape(n_pool, PAGE_SIZE, SEG)
    cv = cache_v.reshape(n_pool, PAGE_SIZE, SEG)
    q3 = qab.reshape(db, ls, SEG)
    pad = ((0, 0), (0, PAGE_SIZE - ls), (0, 0))
    kn = jnp.pad(kab.reshape(db, ls, SEG), pad)
    vn = jnp.pad(vab.reshape(db, ls, SEG), pad)

    def page_spec(g):
        return pl.BlockSpec((None, PAGE_SIZE, SEG), lambda bi, j, pt: (pt[bi * n_pages + j * n_grp + g], 0, 0))

    per_b = pl.BlockSpec((None, ls, SEG), lambda bi, j, pt: (bi, 0, 0))
    new_spec = pl.BlockSpec((None, PAGE_SIZE, SEG), lambda bi, j, pt: (bi, 0, 0))
    rows = 2 * ls * H_A
    grid_spec = pltpu.PrefetchScalarGridSpec(
        num_scalar_prefetch=1,
        grid=(db, n_pages // n_grp),
        in_specs=[pl.BlockSpec((1, 1), lambda bi, j, pt: (0, 0)), per_b]
        + [page_spec(g) for g in range(n_grp)] + [page_spec(g) for g in range(n_grp)]
        + [new_spec, new_spec, pl.BlockSpec((1, DV_A), lambda bi, j, pt: (0, 0))],
        out_specs=per_b,
        scratch_shapes=[pltpu.VMEM((rows, 128), BF16), pltpu.VMEM((rows, 1), F32),
                        pltpu.VMEM((rows, 1), F32), pltpu.VMEM((rows, 128), F32)],
    )
    out = pl.pallas_call(
        functools.partial(_sattn_body, n_grp=n_grp, ls=ls, out_scale=out_scale),
        grid_spec=grid_spec,
        out_shape=jax.ShapeDtypeStruct((db, ls, SEG), F32),
        compiler_params=_cparams("parallel", "arbitrary"),
        name="sample_attn",
    )(page_table.reshape(-1), lam, q3, *([ck] * n_grp), *([cv] * n_grp), kn, vn, diff_ln_w.reshape(1, DV_A))
    return out.reshape(db * ls, SEG)


def _hgrn_body(q_ref, k_ref, v_ref, lf_ref, g_ref, s0_ref, nw_ref, o_ref, sout_ref, st_ref, b_ref,
               *, chunk, sub, n_chunks):
    t = pl.program_id(2)

    @pl.when(t == 0)
    def _():
        st_ref[...] = s0_ref[...].T

    r = lax.broadcasted_iota(jnp.int32, (chunk, chunk), 0)
    c = lax.broadcasted_iota(jnp.int32, (chunk, chunk), 1)
    tri = (c <= r).astype(F32)
    nw = nw_ref[...]

    def one_chunk(ci, carry):
        rows = pl.ds(pl.multiple_of(ci * chunk, chunk), chunk)
        q = q_ref[rows, :]
        k = k_ref[rows, :]
        v = v_ref[rows, :]
        b = jnp.dot(tri, lf_ref[rows, :], precision=lax.Precision.HIGHEST, preferred_element_type=F32)
        b_ref[...] = b
        st = st_ref[...]
        o_parts = []
        for j in range(chunk // sub):
            ref = b_ref[j * sub - 1:j * sub, :] if j > 0 else jnp.zeros((1, DK_B), F32)
            ncol = (j + 1) * sub
            qj = (q[j * sub:ncol] * jnp.exp(b[j * sub:ncol] - ref)).astype(BF16)
            kj = (k[:ncol] * jnp.exp(ref - b[:ncol])).astype(BF16)
            att = _dot_nt(qj, kj)
            causal = (lax.broadcasted_iota(jnp.int32, (sub, ncol), 1)
                      <= lax.broadcasted_iota(jnp.int32, (sub, ncol), 0) + j * sub)
            att = jnp.where(causal, att, 0.0)
            o_parts.append(_dot(att.astype(BF16), v[:ncol]))
        o = jnp.concatenate(o_parts, axis=0) + _dot_nt((q * jnp.exp(b)).astype(BF16), st.astype(BF16))
        b_last = b[chunk - 1:chunk, :]
        k_dec = (k * jnp.exp(b_last - b)).astype(BF16)
        st_ref[...] = st * jnp.exp(b_last) + _dot(v.T, k_dec)
        gate = g_ref[rows, :]
        o_ref[rows, :] = _rms(o, nw) * (gate * jax.nn.sigmoid(gate))
        return carry

    lax.fori_loop(0, n_chunks, one_chunk, 0)

    @pl.when(t == pl.num_programs(2) - 1)
    def _():
        sout_ref[...] = st_ref[...].T


def _hgrn(qb, kb, ib, lf, gb, s0, norm_w, b, l, chunk, n_chunks):
    rows = chunk * n_chunks
    assert l % rows == 0
    sub = min(HGRN_SUB, chunk)
    a3 = lambda t: t.reshape(b, l, SEG)
    tok = pl.BlockSpec((None, rows, 128), lambda bi, h, t: (bi, t, h))
    st = pl.BlockSpec((None, None, DK_B, DV_B), lambda bi, h, t: (bi, h, 0, 0))
    out, s_fin = pl.pallas_call(
        functools.partial(_hgrn_body, chunk=chunk, sub=sub, n_chunks=n_chunks),
        grid=(b, H_B, l // rows),
        in_specs=[tok, tok, tok, tok, tok, st, pl.BlockSpec((1, DV_B), lambda bi, h, t: (0, 0))],
        out_specs=[tok, st],
        out_shape=[jax.ShapeDtypeStruct((b, l, SEG), F32), jax.ShapeDtypeStruct((b, H_B, DK_B, DV_B), F32)],
        scratch_shapes=[pltpu.VMEM((DV_B, DK_B), F32), pltpu.VMEM((chunk, DK_B), F32)],
        compiler_params=_cparams("parallel", "parallel", "arbitrary"),
        name="hgrn2",
    )(a3(qb), a3(kb), a3(ib), a3(lf), a3(gb), s0, norm_w.reshape(1, DV_B))
    return out.reshape(b * l, SEG), s_fin


def _mix_body(x_ref, oa_ref, ob_ref, w_ref, o_ref):
    o_ref[...] = (x_ref[...] + _dot(oa_ref[...].astype(BF16), w_ref[:SEG, :])
                  + _dot(ob_ref[...].astype(BF16), w_ref[SEG:, :]))


def _mix(x2d, oa, ob, w_out_bf, tm):
    n = x2d.shape[0]
    row = lambda i: (i, 0)
    return pl.pallas_call(
        _mix_body,
        grid=(n // tm,),
        in_specs=[pl.BlockSpec((tm, D_MODEL), row), pl.BlockSpec((tm, SEG), row), pl.BlockSpec((tm, SEG), row),
                  pl.BlockSpec((D_MODEL, D_MODEL), lambda i: (0, 0))],
        out_specs=pl.BlockSpec((tm, D_MODEL), row),
        out_shape=jax.ShapeDtypeStruct((n, D_MODEL), F32),
        compiler_params=_cparams("parallel"),
        name="mix_out",
    )(x2d, oa, ob, w_out_bf)


def _memkv_body(m_ref, nw_ref, wk_ref, wv_ref, k_ref, v_ref):
    hb = _rms(m_ref[...], nw_ref[...]).astype(BF16)
    k_ref[...] = _dot(hb, wk_ref[...])
    v_ref[...] = _dot(hb, wv_ref[...])


def _mem_kv(mem2d, norm_w, wk_bf, wv_bf, tm):
    n = mem2d.shape[0]
    row = lambda i: (i, 0)
    const = lambda i: (0, 0)
    o = jax.ShapeDtypeStruct((n, D_MODEL), F32)
    return pl.pallas_call(
        _memkv_body,
        grid=(n // tm,),
        in_specs=[pl.BlockSpec((tm, D_MODEL), row), pl.BlockSpec((1, D_MODEL), const),
                  pl.BlockSpec((D_MODEL, D_MODEL), const), pl.BlockSpec((D_MODEL, D_MODEL), const)],
        out_specs=[pl.BlockSpec((tm, D_MODEL), row)] * 2,
        out_shape=[o, o],
        compiler_params=_cparams("parallel"),
        name="mem_kv",
    )(mem2d, norm_w.reshape(1, D_MODEL), wk_bf, wv_bf)


def _memattn_body(x_ref, nw_ref, wq_ref, mk_ref, mv_ref, wo_ref, o_ref):
    x = x_ref[...]
    q = _dot(_rms(x, nw_ref[...]).astype(BF16), wq_ref[...])
    q = (q * (DH_M ** -0.5)).astype(BF16)
    mk = mk_ref[...].astype(BF16)
    mv = mv_ref[...].astype(BF16)
    heads = []
    for h in range(H_M):
        hl = slice(h * DH_M, (h + 1) * DH_M)
        s = _dot_nt(q[:, hl], mk[:, hl])
        p = jnp.exp(s - jnp.max(s, axis=-1, keepdims=True))
        o = _dot(p.astype(BF16), mv[:, hl])
        heads.append(o / jnp.sum(p, axis=-1, keepdims=True))
    o_ref[...] = x + _dot(jnp.concatenate(heads, axis=1).astype(BF16), wo_ref[...])


def _mem_attend(x2d, norm_w, wq_bf, mk, mv, wo_bf, tm, tiles_per_batch):
    n = x2d.shape[0]
    row = lambda i: (i, 0)
    const = lambda i: (0, 0)
    mem = pl.BlockSpec((None, MEM_LEN, D_MODEL), lambda i: (i // tiles_per_batch, 0, 0))
    return pl.pallas_call(
        _memattn_body,
        grid=(n // tm,),
        in_specs=[pl.BlockSpec((tm, D_MODEL), row), pl.BlockSpec((1, D_MODEL), const),
                  pl.BlockSpec((D_MODEL, D_MODEL), const), mem, mem, pl.BlockSpec((D_MODEL, D_MODEL), const)],
        out_specs=pl.BlockSpec((tm, D_MODEL), row),
        out_shape=jax.ShapeDtypeStruct((n, D_MODEL), F32),
        compiler_params=_cparams("parallel"),
        name="mem_attn",
    )(x2d, norm_w.reshape(1, D_MODEL), wq_bf, mk, mv, wo_bf)


def _extract_topk(work_ref, rank_ref, vals_ref, n_rows, k):
    shape = work_ref.shape
    row = lax.broadcasted_iota(jnp.int32, shape, 0)
    rank_ref[...] = jnp.full(shape, float(k), F32)

    def body(a, carry):
        w = work_ref[...]
        m = jnp.max(w, axis=0, keepdims=True)
        idx = jnp.min(jnp.where(w == m, row, n_rows), axis=0, keepdims=True)
        hit = row == idx
        rank_ref[...] = jnp.where(hit, a.astype(F32), rank_ref[...])
        work_ref[...] = jnp.where(hit, NEG_INF, w)
        vals_ref[pl.ds(a, 1), :] = m
        return carry

    lax.fori_loop(0, k, body, 0)


def _peer_select_body(x_ref, nw_ref, wq_ref, keys_ref, xn_ref, r1_ref, n2_ref, e1_ref, e2_ref,
                      work_ref, rank1_ref, rank2_ref, vals1_ref, vals2_ref, cand_ref, crank_ref, cvals_ref):
    hb = _rms(x_ref[...], nw_ref[...]).astype(BF16)
    xn_ref[...] = hb
    q = _dot(hb, wq_ref[...]).astype(BF16)
    k = PEER_TOPK
    for h in range(PEER_HEADS):
        e_half = []
        for p, (rank_ref, vals_ref) in enumerate(((rank1_ref, vals1_ref), (rank2_ref, vals2_ref))):
            hp = 2 * h + p
            s = _dot_nt(keys_ref[hp], q[:, hp * PEER_HALF:(hp + 1) * PEER_HALF])
            e_half.append(jnp.exp(s - jnp.max(s, axis=0, keepdims=True)))
            work_ref[...] = s
            _extract_topk(work_ref, rank_ref, vals_ref, N_KEYS, k)
        v1 = vals1_ref[...]
        v2 = vals2_ref[...]
        for a in range(k):
            cand_ref[a * k:(a + 1) * k, :] = v1[a:a + 1, :] + v2
        _extract_topk(cand_ref, crank_ref, cvals_ref, k * k, k)
        cv = cvals_ref[...]
        z = jnp.sum(jnp.exp(cv - cv[0:1, :]), axis=0, keepdims=True)
        sel = (crank_ref[...] < float(k)).astype(F32)
        nb = sel[0:k, :]
        for a in range(1, k):
            nb = nb + sel[a * k:(a + 1) * k, :]
        rank2 = rank2_ref[...]
        n2 = jnp.zeros(rank2.shape, F32)
        for b in range(k):
            n2 = jnp.where(rank2 == float(b), nb[b:b + 1, :], n2)
        r1_ref[h] = rank1_ref[...]
        n2_ref[h] = n2
        e1_ref[h] = e_half[0]
        e2_ref[h] = e_half[1] / z


def _peer_select(x2d, norm_w, wq_bf, keys_bf, tt):
    n = x2d.shape[0]
    k = PEER_TOPK
    const2 = lambda i: (0, 0)
    tokmajor = pl.BlockSpec((PEER_HEADS, N_KEYS, tt), lambda i: (0, 0, i))
    kt = jax.ShapeDtypeStruct((PEER_HEADS, N_KEYS, n), F32)
    return pl.pallas_call(
        _peer_select_body,
        grid=(n // tt,),
        in_specs=[pl.BlockSpec((tt, D_MODEL), lambda i: (i, 0)), pl.BlockSpec((1, D_MODEL), const2),
                  pl.BlockSpec(wq_bf.shape, const2), pl.BlockSpec(keys_bf.shape, lambda i: (0, 0, 0))],
        out_specs=[pl.BlockSpec((tt, D_MODEL), lambda i: (i, 0)), tokmajor, tokmajor, tokmajor, tokmajor],
        out_shape=[jax.ShapeDtypeStruct((n, D_MODEL), BF16), kt, kt, kt, kt],
        scratch_shapes=[pltpu.VMEM((N_KEYS, tt), F32), pltpu.VMEM((N_KEYS, tt), F32), pltpu.VMEM((N_KEYS, tt), F32),
                        pltpu.VMEM((k, tt), F32), pltpu.VMEM((k, tt), F32),
                        pltpu.VMEM((k * k, tt), F32), pltpu.VMEM((k * k, tt), F32), pltpu.VMEM((k, tt), F32)],
        compiler_params=_cparams("parallel"),
        name="peer_select",
    )(x2d, norm_w.reshape(1, D_MODEL), wq_bf, keys_bf)


def _peer_dense_body(x_ref, xn_ref, r1_ref, n2_ref, e1_ref, e2_ref, u_ref, vt_ref, fw_ref, o_ref,
                     yt_ref, ht_ref, *, blocks, final_norm):
    c = pl.program_id(1)

    @pl.when(c == 0)
    def _():
        yt_ref[...] = jnp.zeros(yt_ref.shape, F32)

    pre = _dot_nt(u_ref[...], xn_ref[...])
    act = 0.5 * pre * (1.0 + lax.erf(pre * (2.0 ** -0.5)))
    for g in range(blocks):
        i1 = c * blocks + g
        w = None
        for h in range(PEER_HEADS):
            r1 = r1_ref[h, pl.ds(i1, 1), :]
            c1 = e1_ref[h, pl.ds(i1, 1), :]
            term = jnp.where(r1 < n2_ref[h], c1 * e2_ref[h], 0.0)
            w = term if w is None else w + term
        ht_ref[g * N_KEYS:(g + 1) * N_KEYS, :] = (w * act[g * N_KEYS:(g + 1) * N_KEYS, :]).astype(BF16)
    yt_ref[...] += _dot(vt_ref[...], ht_ref[...])

    @pl.when(c == pl.num_programs(1) - 1)
    def _():
        y = x_ref[...] + yt_ref[...].T
        o_ref[...] = _rms(y, fw_ref[...]) if final_norm else y


def _peer_dense(x2d, xn, r1, n2, e1, e2, u_bf, vt_bf, final_w, tt, blocks, final_norm):
    n = x2d.shape[0]
    n_exp = u_bf.shape[0]
    ec = blocks * N_KEYS
    tok = pl.BlockSpec((tt, D_MODEL), lambda i, c: (i, 0))
    sel = pl.BlockSpec((PEER_HEADS, N_KEYS, tt), lambda i, c: (0, 0, i))
    return pl.pallas_call(
        functools.partial(_peer_dense_body, blocks=blocks, final_norm=final_norm),
        grid=(n // tt, n_exp // ec),
        in_specs=[tok, tok, sel, sel, sel, sel,
                  pl.BlockSpec((ec, D_MODEL), lambda i, c: (c, 0)),
                  pl.BlockSpec((D_MODEL, ec), lambda i, c: (0, c)),
                  pl.BlockSpec((1, D_MODEL), lambda i, c: (0, 0))],
        out_specs=tok,
        out_shape=jax.ShapeDtypeStruct((n, D_MODEL), F32),
        scratch_shapes=[pltpu.VMEM((D_MODEL, tt), F32), pltpu.VMEM((ec, tt), BF16)],
        compiler_params=_cparams("parallel", "arbitrary"),
        name="peer_dense",
    )(x2d, xn, r1, n2, e1, e2, u_bf, vt_bf, final_w.reshape(1, D_MODEL))


def _peer(x2d, norm_w, wq_bf, keys_bf, u_bf, vt_bf, final_w, final_norm, t_sel, t_dense, blocks):
    xn, r1, n2, e1, e2 = _peer_select(x2d, norm_w, wq_bf, keys_bf, t_sel)
    return _peer_dense(x2d, xn, r1, n2, e1, e2, u_bf, vt_bf, final_w, t_dense, blocks, final_norm)


def _tile(n, want):
    return min(n, want)


def kernel(x_prompt, x_sample, cache_attn_k, cache_attn_v, cache_mem_k, cache_mem_v, state_hgrn, page_table, mem_prompt, norm_mix_w, w_in, lambda_q1, lambda_k1, lambda_q2, lambda_k2, diff_ln_w, hgrn_lower_bounds, hgrn_norm_w, w_out, norm_mem_q_w, norm_mem_kv_w, w_mq, w_mk, w_mv, w_mo, norm_ffn_w, peer_wq, peer_keys, peer_u, peer_v, final_norm_w):
    b, l = x_prompt.shape[:2]
    db, ls = x_sample.shape[:2]
    depth = w_in.shape[0]
    past = page_table.shape[1] * PAGE_SIZE
    pos_p = jnp.arange(l)
    tm_s = _tile(db * ls, 256)
    pos_s = past + (jnp.arange(tm_s) % ls)
    xp = x_prompt.reshape(b * l, D_MODEL)
    xs = x_sample.reshape(db * ls, D_MODEL)
    chunk_p = HGRN_CHUNK if l % HGRN_CHUNK == 0 else l
    chunk_s = HGRN_CHUNK if ls % HGRN_CHUNK == 0 else ls
    tq = _tile(l, 512)
    outs = [[] for _ in range(8)]
    for layer in range(depth):
        lam_init = 0.8 - 0.6 * math.exp(-0.3 * layer)
        lam = (jnp.exp(jnp.sum(lambda_q1[layer] * lambda_k1[layer]))
               - jnp.exp(jnp.sum(lambda_q2[layer] * lambda_k2[layer])) + lam_init).reshape(1, 1).astype(F32)
        last = layer == depth - 1
        w_in_bf = w_in[layer].astype(BF16)
        w_out_bf = w_out[layer].astype(BF16)
        wmq, wmk, wmv, wmo = (w[layer].astype(BF16) for w in (w_mq, w_mk, w_mv, w_mo))
        wq_bf = peer_wq[layer].astype(BF16)
        keys_bf = peer_keys[layer].astype(BF16).reshape(PEER_HEADS * 2, N_KEYS, PEER_HALF)
        u_bf = peer_u[layer].astype(BF16)
        vt_bf = peer_v[layer].astype(BF16).T
        peer = functools.partial(_peer, norm_w=norm_ffn_w[layer], wq_bf=wq_bf, keys_bf=keys_bf, u_bf=u_bf,
                                 vt_bf=vt_bf, final_w=final_norm_w, final_norm=last)

        ka, va, qab, kab, vab, qb, kb, ib, lf, gb = _inproj(
            xp, norm_mix_w[layer], w_in_bf, hgrn_lower_bounds, pos_p, layer, _tile(l, 256))
        oa = _prompt_attention(lam, qab, kab, vab, diff_ln_w[layer], 1.0 - lam_init, b, l, tq, tq)
        n_chunks = max(1, min(l, 512) // chunk_p)
        ob, sp = _hgrn(qb, kb, ib, lf, gb, jnp.zeros((b, H_B, DK_B, DV_B), F32), hgrn_norm_w[layer],
                       b, l, chunk_p, n_chunks)
        xp = _mix(xp, oa, ob, w_out_bf, _tile(l, 512))
        mk, mv = _mem_kv(mem_prompt.reshape(b * MEM_LEN, D_MODEL), norm_mem_kv_w[layer], wmk, wmv, MEM_LEN)
        tm = _tile(l, 512)
        xp = _mem_attend(xp, norm_mem_q_w[layer], wmq, mk.reshape(b, MEM_LEN, D_MODEL),
                         mv.reshape(b, MEM_LEN, D_MODEL), wmo, tm, l // tm)
        xp = peer(xp, t_sel=_tile(b * l, 256), t_dense=_tile(b * l, 512), blocks=8)
        outs[0].append(ka.reshape(b, l, H_A, 2, DK_A))
        outs[1].append(va.reshape(b, l, H_A, DV_A))
        outs[2].append(sp)
        outs[3].append(mk.reshape(b, MEM_LEN, H_M, DH_M))
        outs[4].append(mv.reshape(b, MEM_LEN, H_M, DH_M))

        ka, va, qab, kab, vab, qb, kb, ib, lf, gb = _inproj(
            xs, norm_mix_w[layer], w_in_bf, hgrn_lower_bounds, pos_s, layer, tm_s)
        oa = _sample_attention(lam, qab, kab, vab, cache_attn_k[layer], cache_attn_v[layer], page_table,
                               diff_ln_w[layer], 1.0 - lam_init, db, ls, min(8, page_table.shape[1]))
        ob, ss = _hgrn(qb, kb, ib, lf, gb, state_hgrn[layer], hgrn_norm_w[layer], db, ls, chunk_s, ls // chunk_s)
        xs = _mix(xs, oa, ob, w_out_bf, tm_s)
        xs = _mem_attend(xs, norm_mem_q_w[layer], wmq, cache_mem_k[layer].reshape(db, MEM_LEN, D_MODEL),
                         cache_mem_v[layer].reshape(db, MEM_LEN, D_MODEL), wmo, ls, 1)
        xs = peer(xs, t_sel=tm_s, t_dense=_tile(db * ls, 512), blocks=8)
        outs[5].append(ka.reshape(db, ls, H_A, 2, DK_A))
        outs[6].append(va.reshape(db, ls, H_A, DV_A))
        outs[7].append(ss)
    y_prompt = xp.reshape(b, l, D_MODEL)
    y_sample = xs.reshape(db, ls, D_MODEL)
    return (y_prompt, y_sample) + tuple(jnp.stack(o) for o in outs)
```

```python
import functools
import math

import jax
import jax.numpy as jnp
from jax import lax
from jax.experimental import pallas as pl
from jax.experimental.pallas import tpu as pltpu

F32 = jnp.float32
BF16 = jnp.bfloat16
EPS = 1e-6
NEG_INF = float("-inf")

D_MODEL = 1024
PAGE_SIZE = 128
H_A = 4
DV_A = 128
DK_A = 64
ROT_DIM = 16
ROPE_THETA = 500000.0
H_B = 4
DK_B = 128
DV_B = 128
HGRN_CHUNK = 64
HGRN_SUB = 16
MEM_LEN = 256
H_M = 4
DH_M = 256
N_KEYS = 128
PEER_HEADS = 8
PEER_TOPK = 16
PEER_HALF = 128
SEG = 512
N_SEG = 7

VMEM_LIMIT = 56 * 1024 * 1024


def _cparams(*sem, flags=None):
    return pltpu.CompilerParams(dimension_semantics=sem, vmem_limit_bytes=VMEM_LIMIT, flags=flags)


def _rms(x, w):
    return x * lax.rsqrt(jnp.mean(x * x, axis=-1, keepdims=True) + EPS) * w


def _dot_nt(a, b):
    return lax.dot_general(a, b, (((1,), (1,)), ((), ())), preferred_element_type=F32)


def _dot(a, b):
    return jnp.dot(a, b, preferred_element_type=F32)


def _inproj_body(x_ref, nw_ref, w_ref, lbp_ref, rc_ref, rs1_ref, rs2_ref,
                 ka_ref, va_ref, qab_ref, kab_ref, vab_ref,
                 qb_ref, kb_ref, ib_ref, lf_ref, gb_ref, *, layer, transposed):
    hb = _rms(x_ref[...], nw_ref[...]).astype(BF16)

    def seg(i):
        return _dot(hb, w_ref[:, i * SEG:(i + 1) * SEG])

    rc, rs1, rs2 = rc_ref[...], rs1_ref[...], rs2_ref[...]

    def rope_block(blk):
        return blk * rc + pltpu.roll(blk, 128 - ROT_DIM // 2, 1) * rs1 + pltpu.roll(blk, ROT_DIM // 2, 1) * rs2

    qa = seg(0)
    ka = seg(1)
    for g in range(SEG // 128):
        sl = slice(g * 128, (g + 1) * 128)
        qr = rope_block(qa[:, sl]) * (DK_A ** -0.5)
        if transposed:
            qab_ref[sl, :] = qr.T.astype(BF16)
        else:
            qab_ref[:, sl] = qr.astype(BF16)
        kr = rope_block(ka[:, sl])
        ka_ref[:, sl] = kr
        kab_ref[:, sl] = kr.astype(BF16)
    va = seg(2)
    va_ref[...] = va
    if transposed:
        for g in range(SEG // 128):
            sl = slice(g * 128, (g + 1) * 128)
            vab_ref[sl, :] = va[:, sl].T.astype(BF16)
    else:
        vab_ref[...] = va.astype(BF16)
    qb_ref[...] = seg(3)
    lbp = lbp_ref[...]
    e = jnp.exp(lbp - jnp.max(lbp, axis=0, keepdims=True))
    lb = jnp.sum(e[:layer + 1], axis=0, keepdims=True) / jnp.sum(e, axis=0, keepdims=True)
    fg = lb + (1.0 - lb) * jax.nn.sigmoid(seg(4))
    lf_ref[...] = jnp.log(fg)
    kb_ref[...] = 1.0 - fg
    ib_ref[...] = seg(5).astype(BF16)
    gb_ref[...] = seg(6)


def _rope_tables(pos):
    half = ROT_DIM // 2
    inv = ROPE_THETA ** (-jnp.arange(half, dtype=F32) * 2.0 / ROT_DIM)
    ang = pos.astype(F32)[:, None] * inv[None, :]
    cos, sin = jnp.cos(ang), jnp.sin(ang)
    p = pos.shape[0]
    ones = jnp.ones((p, DK_A - ROT_DIM), F32)
    zeros = jnp.zeros((p, DK_A - ROT_DIM), F32)
    zh = jnp.zeros((p, half), F32)
    rc = jnp.concatenate([cos, cos, ones], axis=1)
    rs1 = jnp.concatenate([-sin, zh, zeros], axis=1)
    rs2 = jnp.concatenate([zh, sin, zeros], axis=1)
    return tuple(jnp.concatenate([t, t], axis=1) for t in (rc, rs1, rs2))


def _inproj(x2d, norm_w, w_in_bf, lower_bounds, pos, layer, tm, transposed):
    n = x2d.shape[0]
    p = pos.shape[0]
    assert n % tm == 0 and p % tm == 0
    npb = p // tm
    rc, rs1, rs2 = _rope_tables(pos)
    row = lambda i: (i, 0)
    const = lambda i: (0, 0)
    tab = lambda i: (i % npb, 0)
    f32o = jax.ShapeDtypeStruct((n, SEG), F32)
    bf16o = jax.ShapeDtypeStruct((n, SEG), BF16)
    out_spec = pl.BlockSpec((tm, SEG), row)
    out_specs = [out_spec] * 10
    out_shape = [f32o, f32o, bf16o, bf16o, bf16o, f32o, f32o, bf16o, f32o, f32o]
    if transposed:
        t_spec = pl.BlockSpec((None, SEG, tm), lambda i: (i, 0, 0))
        t_shape = jax.ShapeDtypeStruct((n // tm, SEG, tm), BF16)
        out_specs[2] = out_specs[4] = t_spec
        out_shape[2] = out_shape[4] = t_shape
    return pl.pallas_call(
        functools.partial(_inproj_body, layer=layer, transposed=transposed),
        grid=(n // tm,),
        in_specs=[
            pl.BlockSpec((tm, D_MODEL), row),
            pl.BlockSpec((1, D_MODEL), const),
            pl.BlockSpec((D_MODEL, N_SEG * SEG), const),
            pl.BlockSpec(lower_bounds.shape, const),
            pl.BlockSpec((tm, 128), tab),
            pl.BlockSpec((tm, 128), tab),
            pl.BlockSpec((tm, 128), tab),
        ],
        out_specs=out_specs,
        out_shape=out_shape,
        compiler_params=_cparams("parallel"),
        name="inproj",
    )(x2d, norm_w.reshape(1, D_MODEL), w_in_bf, lower_bounds, rc, rs1, rs2)


def _split_maps(q):
    lane = lax.broadcasted_iota(jnp.int32, q.shape, 1)
    zero = jnp.zeros_like(q)
    return jnp.concatenate([jnp.where(lane < DK_A, q, zero), jnp.where(lane >= DK_A, q, zero)], axis=0)


def _online_update(s, v, m_ref, l_ref, acc_ref):
    m_prev = m_ref[...]
    m_new = jnp.maximum(m_prev, jnp.max(s, axis=-1, keepdims=True))
    alpha = jnp.exp(m_prev - m_new)
    p = jnp.exp(s - m_new)
    l_ref[...] = alpha * l_ref[...] + jnp.sum(p, axis=-1, keepdims=True)
    acc_ref[...] = alpha * acc_ref[...] + _dot(p.astype(BF16), v)
    m_ref[...] = m_new


def _diff_finish(acc, l, t, lam, lnw, out_scale):
    o = acc[:t] / l[:t] - lam * (acc[t:] / l[t:])
    return _rms(o, lnw) * out_scale


def _pattn_body(lam_ref, qt_ref, k_ref, vt_ref, lnw_ref, o_ref, m_ref, l_ref, acc_ref, *, t, out_scale):
    i = pl.program_id(2)
    qt = qt_ref[...]
    feat = lax.broadcasted_iota(jnp.int32, qt.shape, 0)
    zero = jnp.zeros_like(qt)
    q2t = jnp.concatenate([jnp.where(feat < DK_A, qt, zero), jnp.where(feat >= DK_A, qt, zero)], axis=1)
    m_ref[...] = jnp.full(m_ref.shape, NEG_INF, F32)
    l_ref[...] = jnp.zeros(l_ref.shape, F32)
    acc_ref[...] = jnp.zeros(acc_ref.shape, F32)

    def block(j, masked):
        s = _dot(k_ref[j], q2t)
        if masked:
            key = lax.broadcasted_iota(jnp.int32, s.shape, 0) + j * t
            qp = lax.broadcasted_iota(jnp.int32, s.shape, 1)
            qp = jnp.where(qp >= t, qp - t, qp) + i * t
            s = jnp.where(key <= qp, s, NEG_INF)
        m_prev = m_ref[...]
        m_new = jnp.maximum(m_prev, jnp.max(s, axis=0, keepdims=True))
        alpha = jnp.exp(m_prev - m_new)
        p = jnp.exp(s - m_new)
        l_ref[...] = alpha * l_ref[...] + jnp.sum(p, axis=0, keepdims=True)
        acc_ref[...] = alpha * acc_ref[...] + _dot(vt_ref[j], p.astype(BF16))
        m_ref[...] = m_new

    def full_block(j, carry):
        block(j, False)
        return carry

    lax.fori_loop(0, i, full_block, 0)
    block(i, True)
    acc = acc_ref[...]
    l = l_ref[...]
    ot = acc[:, :t] / l[:, :t] - lam_ref[...] * (acc[:, t:] / l[:, t:])
    ot = ot * lax.rsqrt(jnp.mean(ot * ot, axis=0, keepdims=True) + EPS)
    o_ref[...] = ot.T * lnw_ref[...] * out_scale


def _prompt_attention(lam, qt, kab, vt, diff_ln_w, out_scale, b, l, t):
    assert l % t == 0
    nq = l // t
    k3 = kab.reshape(b * nq, t, SEG)
    out = pl.pallas_call(
        functools.partial(_pattn_body, t=t, out_scale=out_scale),
        grid=(b, H_A, nq),
        in_specs=[
            pl.BlockSpec((1, 1), lambda bi, h, i: (0, 0)),
            pl.BlockSpec((None, 128, t), lambda bi, h, i: (bi * nq + i, h, 0)),
            pl.BlockSpec((nq, t, 128), lambda bi, h, i: (bi, 0, h)),
            pl.BlockSpec((nq, 128, t), lambda bi, h, i: (bi, h, 0)),
            pl.BlockSpec((1, DV_A), lambda bi, h, i: (0, 0)),
        ],
        out_specs=pl.BlockSpec((None, t, 128), lambda bi, h, i: (bi, i, h)),
        out_shape=jax.ShapeDtypeStruct((b, l, SEG), F32),
        scratch_shapes=[pltpu.VMEM((1, 2 * t), F32), pltpu.VMEM((1, 2 * t), F32), pltpu.VMEM((128, 2 * t), F32)],
        compiler_params=_cparams("parallel", "parallel", "arbitrary"),
        name="prompt_attn",
    )(lam, qt, k3, vt, diff_ln_w.reshape(1, DV_A))
    return out.reshape(b * l, SEG)


def _sattn_body(pt_ref, lam_ref, q_ref, *refs, n_grp, ls, out_scale):
    k_refs = refs[:n_grp]
    v_refs = refs[n_grp:2 * n_grp]
    kn_ref, vn_ref, lnw_ref, o_ref, q2_ref, m_ref, l_ref, acc_ref = refs[2 * n_grp:]
    j = pl.program_id(1)
    rows = 2 * ls

    @pl.when(j == 0)
    def _():
        q = q_ref[...]
        for h in range(H_A):
            q2_ref[h * rows:(h + 1) * rows, :] = _split_maps(q[:, h * 128:(h + 1) * 128])
        m_ref[...] = jnp.full(m_ref.shape, NEG_INF, F32)
        l_ref[...] = jnp.zeros(l_ref.shape, F32)
        acc_ref[...] = jnp.zeros(acc_ref.shape, F32)

    def head_update(h, kt, v, mask):
        hs = slice(h * rows, (h + 1) * rows)
        s = _dot(q2_ref[hs, :], kt)
        if mask is not None:
            s = jnp.where(mask, s, NEG_INF)
        _online_update(s, v, m_ref.at[hs, :], l_ref.at[hs, :], acc_ref.at[hs, :])

    for h in range(H_A):
        hl = slice(h * 128, (h + 1) * 128)
        head_update(h, jnp.concatenate([r[hl, :].astype(BF16) for r in k_refs], axis=1),
                    jnp.concatenate([r[pl.ds(h, PAGE_SIZE, stride=H_A), :].astype(BF16) for r in v_refs], axis=0),
                    None)

    @pl.when(j == pl.num_programs(1) - 1)
    def _():
        knt = kn_ref[...]
        vn = vn_ref[...]
        r = lax.broadcasted_iota(jnp.int32, (rows, PAGE_SIZE), 0)
        r = jnp.where(r >= ls, r - ls, r)
        c = lax.broadcasted_iota(jnp.int32, (rows, PAGE_SIZE), 1)
        mask = c <= r
        for h in range(H_A):
            hl = slice(h * 128, (h + 1) * 128)
            hs = slice(h * rows, (h + 1) * rows)
            head_update(h, knt[hl, :], vn[:, hl], mask)
            o_ref[:, hl] = _diff_finish(acc_ref[hs, :], l_ref[hs, :], ls, lam_ref[...], lnw_ref[...], out_scale)


def _sample_attention(lam, qab, kab, vab, cache_k, cache_v, page_table, diff_ln_w, out_scale, db, ls, n_grp):
    n_pages = page_table.shape[1]
    assert n_pages % n_grp == 0
    n_pool = cache_k.shape[0]
    ck = jnp.transpose(cache_k, (0, 2, 3, 4, 1)).reshape(n_pool, SEG, PAGE_SIZE)
    cv = cache_v.reshape(n_pool, PAGE_SIZE * H_A, DV_A)
    q3 = qab.reshape(db, ls, SEG)
    kn = jnp.pad(jnp.transpose(kab.reshape(db, ls, SEG), (0, 2, 1)), ((0, 0), (0, 0), (0, PAGE_SIZE - ls)))
    vn = jnp.pad(vab.reshape(db, ls, SEG), ((0, 0), (0, PAGE_SIZE - ls), (0, 0)))

    def page_spec(g, shape):
        return pl.BlockSpec((None,) + shape, lambda bi, j, pt: (pt[bi * n_pages + j * n_grp + g], 0, 0))

    per_b = pl.BlockSpec((None, ls, SEG), lambda bi, j, pt: (bi, 0, 0))
    rows = 2 * ls * H_A
    grid_spec = pltpu.PrefetchScalarGridSpec(
        num_scalar_prefetch=1,
        grid=(db, n_pages // n_grp),
        in_specs=[pl.BlockSpec((1, 1), lambda bi, j, pt: (0, 0)), per_b]
        + [page_spec(g, (SEG, PAGE_SIZE)) for g in range(n_grp)]
        + [page_spec(g, (PAGE_SIZE * H_A, DV_A)) for g in range(n_grp)]
        + [pl.BlockSpec((None, SEG, PAGE_SIZE), lambda bi, j, pt: (bi, 0, 0)),
           pl.BlockSpec((None, PAGE_SIZE, SEG), lambda bi, j, pt: (bi, 0, 0)),
           pl.BlockSpec((1, DV_A), lambda bi, j, pt: (0, 0))],
        out_specs=per_b,
        scratch_shapes=[pltpu.VMEM((rows, 128), BF16), pltpu.VMEM((rows, 1), F32),
                        pltpu.VMEM((rows, 1), F32), pltpu.VMEM((rows, 128), F32)],
    )
    out = pl.pallas_call(
        functools.partial(_sattn_body, n_grp=n_grp, ls=ls, out_scale=out_scale),
        grid_spec=grid_spec,
        out_shape=jax.ShapeDtypeStruct((db, ls, SEG), F32),
        compiler_params=_cparams("parallel", "arbitrary"),
        name="sample_attn",
    )(page_table.reshape(-1), lam, q3, *([ck] * n_grp), *([cv] * n_grp), kn, vn, diff_ln_w.reshape(1, DV_A))
    return out.reshape(db * ls, SEG)


def _hgrn_body(q_ref, k_ref, v_ref, lf_ref, g_ref, s0_ref, nw_ref, o_ref, sout_ref, st_ref, b_ref,
               *, chunk, sub, n_chunks):
    t = pl.program_id(1)

    @pl.when(t == 0)
    def _():
        for h in range(H_B):
            st_ref[h] = s0_ref[h].T

    r = lax.broadcasted_iota(jnp.int32, (chunk, chunk), 0)
    c = lax.broadcasted_iota(jnp.int32, (chunk, chunk), 1)
    tri = (c <= r).astype(F32)
    nw = nw_ref[...]

    n_sub = chunk // sub
    heads = range(H_B)
    hl = [slice(h * DK_B, (h + 1) * DK_B) for h in heads]

    def one_chunk(ci, carry):
        rows = pl.ds(pl.multiple_of(ci * chunk, chunk), chunk)
        b_ref[...] = jnp.dot(tri, lf_ref[rows, :], precision=lax.Precision.HIGHEST, preferred_element_type=F32)
        q = [q_ref[rows, hl[h]] for h in heads]
        k = [k_ref[rows, hl[h]] for h in heads]
        v = [v_ref[rows, hl[h]] for h in heads]
        b = [b_ref[:, hl[h]] for h in heads]
        st = [st_ref[h] for h in heads]
        att = {}
        for h in heads:
            for j in range(n_sub):
                ref = b_ref[j * sub - 1:j * sub, hl[h]] if j > 0 else jnp.zeros((1, DK_B), F32)
                ncol = (j + 1) * sub
                qj = (q[h][j * sub:ncol] * jnp.exp(b[h][j * sub:ncol] - ref)).astype(BF16)
                kj = (k[h][:ncol] * jnp.exp(ref - b[h][:ncol])).astype(BF16)
                att[h, j] = _dot_nt(qj, kj)
        inter = [_dot_nt((q[h] * jnp.exp(b[h])).astype(BF16), st[h].astype(BF16)) for h in heads]
        b_last = [b[h][chunk - 1:chunk, :] for h in heads]
        upd = [_dot(v[h].T, (k[h] * jnp.exp(b_last[h] - b[h])).astype(BF16)) for h in heads]
        for h in heads:
            o_parts = []
            for j in range(n_sub):
                ncol = (j + 1) * sub
                causal = (lax.broadcasted_iota(jnp.int32, (sub, ncol), 1)
                          <= lax.broadcasted_iota(jnp.int32, (sub, ncol), 0) + j * sub)
                a = jnp.where(causal, att[h, j], 0.0)
                o_parts.append(_dot(a.astype(BF16), v[h][:ncol]))
            o = jnp.concatenate(o_parts, axis=0) + inter[h]
            st_ref[h] = st[h] * jnp.exp(b_last[h]) + upd[h]
            gate = g_ref[rows, hl[h]]
            o_ref[rows, hl[h]] = _rms(o, nw) * (gate * jax.nn.sigmoid(gate))
        return carry

    lax.fori_loop(0, n_chunks, one_chunk, 0)

    @pl.when(t == pl.num_programs(1) - 1)
    def _():
        for h in range(H_B):
            sout_ref[h] = st_ref[h].T


def _hgrn(qb, kb, ib, lf, gb, s0, norm_w, b, l, chunk, n_chunks):
    rows = chunk * n_chunks
    assert l % rows == 0
    sub = min(HGRN_SUB, chunk)
    a3 = lambda t: t.reshape(b, l, SEG)
    tok = pl.BlockSpec((None, rows, SEG), lambda bi, t: (bi, t, 0))
    st = pl.BlockSpec((None, H_B, DK_B, DV_B), lambda bi, t: (bi, 0, 0, 0))
    out, s_fin = pl.pallas_call(
        functools.partial(_hgrn_body, chunk=chunk, sub=sub, n_chunks=n_chunks),
        grid=(b, l // rows),
        in_specs=[tok, tok, tok, tok, tok, st, pl.BlockSpec((1, DV_B), lambda bi, t: (0, 0))],
        out_specs=[tok, st],
        out_shape=[jax.ShapeDtypeStruct((b, l, SEG), F32), jax.ShapeDtypeStruct((b, H_B, DK_B, DV_B), F32)],
        scratch_shapes=[pltpu.VMEM((H_B, DV_B, DK_B), F32), pltpu.VMEM((chunk, SEG), F32)],
        compiler_params=_cparams("parallel", "arbitrary"),
        name="hgrn2",
    )(a3(qb), a3(kb), a3(ib), a3(lf), a3(gb), s0, norm_w.reshape(1, DV_B))
    return out.reshape(b * l, SEG), s_fin


def _mix_body(x_ref, oa_ref, ob_ref, w_ref, o_ref):
    o_ref[...] = (x_ref[...] + _dot(oa_ref[...].astype(BF16), w_ref[:SEG, :])
                  + _dot(ob_ref[...].astype(BF16), w_ref[SEG:, :]))


def _mix(x2d, oa, ob, w_out_bf, tm):
    n = x2d.shape[0]
    row = lambda i: (i, 0)
    return pl.pallas_call(
        _mix_body,
        grid=(n // tm,),
        in_specs=[pl.BlockSpec((tm, D_MODEL), row), pl.BlockSpec((tm, SEG), row), pl.BlockSpec((tm, SEG), row),
                  pl.BlockSpec((D_MODEL, D_MODEL), lambda i: (0, 0))],
        out_specs=pl.BlockSpec((tm, D_MODEL), row),
        out_shape=jax.ShapeDtypeStruct((n, D_MODEL), F32),
        compiler_params=_cparams("parallel"),
        name="mix_out",
    )(x2d, oa, ob, w_out_bf)


def _memkv_body(m_ref, nw_ref, wk_ref, wv_ref, k_ref, v_ref):
    hb = _rms(m_ref[...], nw_ref[...]).astype(BF16)
    k_ref[...] = _dot(hb, wk_ref[...])
    v_ref[...] = _dot(hb, wv_ref[...])


def _mem_kv(mem2d, norm_w, wk_bf, wv_bf, tm):
    n = mem2d.shape[0]
    row = lambda i: (i, 0)
    const = lambda i: (0, 0)
    o = jax.ShapeDtypeStruct((n, D_MODEL), F32)
    return pl.pallas_call(
        _memkv_body,
        grid=(n // tm,),
        in_specs=[pl.BlockSpec((tm, D_MODEL), row), pl.BlockSpec((1, D_MODEL), const),
                  pl.BlockSpec((D_MODEL, D_MODEL), const), pl.BlockSpec((D_MODEL, D_MODEL), const)],
        out_specs=[pl.BlockSpec((tm, D_MODEL), row)] * 2,
        out_shape=[o, o],
        compiler_params=_cparams("parallel"),
        name="mem_kv",
    )(mem2d, norm_w.reshape(1, D_MODEL), wk_bf, wv_bf)


def _memattn_body(x_ref, nw_ref, wq_ref, mk_ref, mv_ref, wo_ref, o_ref):
    x = x_ref[...]
    q = _dot(_rms(x, nw_ref[...]).astype(BF16), wq_ref[...])
    q = (q * (DH_M ** -0.5)).astype(BF16)
    mk = mk_ref[...].astype(BF16)
    mv = mv_ref[...].astype(BF16)
    heads = []
    for h in range(H_M):
        hl = slice(h * DH_M, (h + 1) * DH_M)
        s = _dot_nt(q[:, hl], mk[:, hl])
        p = jnp.exp(s - jnp.max(s, axis=-1, keepdims=True))
        o = _dot(p.astype(BF16), mv[:, hl])
        heads.append(o / jnp.sum(p, axis=-1, keepdims=True))
    o_ref[...] = x + _dot(jnp.concatenate(heads, axis=1).astype(BF16), wo_ref[...])


def _mem_attend(x2d, norm_w, wq_bf, mk, mv, wo_bf, tm, tiles_per_batch):
    n = x2d.shape[0]
    row = lambda i: (i, 0)
    const = lambda i: (0, 0)
    mem = pl.BlockSpec((None, MEM_LEN, D_MODEL), lambda i: (i // tiles_per_batch, 0, 0))
    return pl.pallas_call(
        _memattn_body,
        grid=(n // tm,),
        in_specs=[pl.BlockSpec((tm, D_MODEL), row), pl.BlockSpec((1, D_MODEL), const),
                  pl.BlockSpec((D_MODEL, D_MODEL), const), mem, mem, pl.BlockSpec((D_MODEL, D_MODEL), const)],
        out_specs=pl.BlockSpec((tm, D_MODEL), row),
        out_shape=jax.ShapeDtypeStruct((n, D_MODEL), F32),
        compiler_params=_cparams("parallel"),
        name="mem_attn",
    )(x2d, norm_w.reshape(1, D_MODEL), wq_bf, mk, mv, wo_bf)


_REMOVED_EXP = 100
_REMOVED_BITS = ((_REMOVED_EXP + 127) << 23) - (1 << 31)


def _extract_topk(work_ref, rank_ref, vals_ref, n_rows, k):
    shape = work_ref.shape
    row = lax.broadcasted_iota(jnp.int32, shape, 0)

    def body(a, carry):
        w = work_ref[...]
        m = jnp.max(w, axis=0, keepdims=True)
        idx = jnp.min(jnp.where(w == m, row, n_rows), axis=0, keepdims=True)
        marker = pltpu.bitcast(jnp.full(shape, _REMOVED_BITS, jnp.int32) + (a << 23), F32)
        work_ref[...] = jnp.where(row == idx, marker, w)
        vals_ref[pl.ds(a, 1), :] = m
        return carry

    lax.fori_loop(0, k, body, 0)
    bits = pltpu.bitcast(work_ref[...], jnp.int32)
    order = ((bits >> 23) & 0xFF) - (_REMOVED_EXP + 127)
    removed = work_ref[...] <= -(2.0 ** _REMOVED_EXP)
    rank_ref[...] = jnp.where(removed, order, k).astype(F32)


def _peer_select_body(x_ref, nw_ref, wq_ref, keys_ref, xn_ref, r1_ref, n2_ref, e1_ref, e2_ref,
                      work_ref, rank1_ref, rank2_ref, vals1_ref, vals2_ref, cand_ref, crank_ref, cvals_ref):
    hbt = _rms(x_ref[...], nw_ref[...]).T.astype(BF16)
    xn_ref[...] = hbt
    qt = _dot(wq_ref[...], hbt).astype(BF16)
    k = PEER_TOPK
    for h in range(PEER_HEADS):
        e_half = []
        for p, (rank_ref, vals_ref) in enumerate(((rank1_ref, vals1_ref), (rank2_ref, vals2_ref))):
            hp = 2 * h + p
            s = _dot(keys_ref[hp], qt[hp * PEER_HALF:(hp + 1) * PEER_HALF, :])
            e_half.append(jnp.exp(s - jnp.max(s, axis=0, keepdims=True)))
            work_ref[...] = s
            _extract_topk(work_ref, rank_ref, vals_ref, N_KEYS, k)
        v1 = vals1_ref[...]
        v2 = vals2_ref[...]
        cand_ref[...] = jnp.full(cand_ref.shape, NEG_INF, F32)
        for a, (off, nb_a) in enumerate(_CAND_ROWS):
            cand_ref[off:off + nb_a, :] = v1[a:a + 1, :] + v2[:nb_a, :]
        _extract_topk(cand_ref, crank_ref, cvals_ref, _N_CAND_PAD, k)
        cv = cvals_ref[...]
        z = jnp.sum(jnp.exp(cv - cv[0:1, :]), axis=0, keepdims=True)
        sel = (crank_ref[...] < float(k)).astype(F32)
        tt = sel.shape[1]
        nb = None
        for off, nb_a in _CAND_ROWS:
            rows = sel[off:off + nb_a, :]
            if nb_a < k:
                rows = jnp.concatenate([rows, jnp.zeros((k - nb_a, tt), F32)], axis=0)
            nb = rows if nb is None else nb + rows
        rank2 = rank2_ref[...]
        n2 = jnp.zeros(rank2.shape, F32)
        for b in range(k):
            n2 = jnp.where(rank2 == float(b), nb[b:b + 1, :], n2)
        r1_ref[h] = rank1_ref[...]
        n2_ref[h] = n2.astype(BF16)
        e1_ref[h] = e_half[0]
        e2_ref[h] = (e_half[1] / z).astype(BF16)


def _cand_rows(k):
    rows, off = [], 0
    for a in range(k):
        nb_a = k // (a + 1)
        rows.append((off, nb_a))
        off += nb_a
    return tuple(rows), off


_CAND_ROWS, _N_CAND = _cand_rows(PEER_TOPK)
_N_CAND_PAD = -(-_N_CAND // 8) * 8


def _peer_select(x2d, norm_w, wq_bf, keys_bf, tt):
    n = x2d.shape[0]
    k = PEER_TOPK
    const2 = lambda i: (0, 0)
    tokmajor = pl.BlockSpec((PEER_HEADS, N_KEYS, tt), lambda i: (0, 0, i))
    kt = jax.ShapeDtypeStruct((PEER_HEADS, N_KEYS, n), F32)
    kt16 = jax.ShapeDtypeStruct((PEER_HEADS, N_KEYS, n), BF16)
    return pl.pallas_call(
        _peer_select_body,
        grid=(n // tt,),
        in_specs=[pl.BlockSpec((tt, D_MODEL), lambda i: (i, 0)), pl.BlockSpec((1, D_MODEL), const2),
                  pl.BlockSpec(wq_bf.shape, const2), pl.BlockSpec(keys_bf.shape, lambda i: (0, 0, 0))],
        out_specs=[pl.BlockSpec((D_MODEL, tt), lambda i: (0, i)), tokmajor, tokmajor, tokmajor, tokmajor],
        out_shape=[jax.ShapeDtypeStruct((D_MODEL, n), BF16), kt, kt16, kt, kt16],
        scratch_shapes=[pltpu.VMEM((N_KEYS, tt), F32), pltpu.VMEM((N_KEYS, tt), F32), pltpu.VMEM((N_KEYS, tt), F32),
                        pltpu.VMEM((k, tt), F32), pltpu.VMEM((k, tt), F32),
                        pltpu.VMEM((_N_CAND_PAD, tt), F32), pltpu.VMEM((_N_CAND_PAD, tt), F32),
                        pltpu.VMEM((k, tt), F32)],
        compiler_params=_cparams("parallel"),
        name="peer_select",
    )(x2d, norm_w.reshape(1, D_MODEL), wq_bf, keys_bf)


def _peer_dense_body(x_ref, xnt_ref, r1_ref, n2_ref, e1_ref, e2_ref, u_ref, vt_ref, fw_ref, o_ref,
                     yt_ref, ht_ref, *, blocks, final_norm):
    c = pl.program_id(1)

    @pl.when(c == 0)
    def _():
        yt_ref[...] = jnp.zeros(yt_ref.shape, F32)

    tt = xnt_ref.shape[1]
    pre = _dot(u_ref[...], xnt_ref[...])
    act = (0.5 * pre * (1.0 + lax.erf(pre * (2.0 ** -0.5)))).astype(BF16)
    reps = N_KEYS // _ROW_BCAST

    def bcast_row(ref, h, g):
        row = jnp.broadcast_to(ref[h, g:g + 1, :], (_ROW_BCAST, tt)).astype(BF16)
        return jnp.concatenate([row] * reps, axis=0)

    for g in range(blocks):
        w = None
        for h in range(PEER_HEADS):
            r1 = bcast_row(r1_ref, h, g)
            gate = bcast_row(e1_ref, h, g) * e2_ref[h]
            term = jnp.where(r1 < n2_ref[h], gate, jnp.zeros_like(gate))
            w = term if w is None else w + term
        ht_ref[g * N_KEYS:(g + 1) * N_KEYS, :] = w * act[g * N_KEYS:(g + 1) * N_KEYS, :]
    yt_ref[...] += _dot(vt_ref[...], ht_ref[...])

    @pl.when(c == pl.num_programs(1) - 1)
    def _():
        y = x_ref[...] + yt_ref[...].T
        o_ref[...] = _rms(y, fw_ref[...]) if final_norm else y


_ROW_BCAST = 16


def _peer_dense(x2d, xnt, r1, n2, e1, e2, u_bf, vt_bf, final_w, tt, blocks, final_norm):
    n = x2d.shape[0]
    n_exp = u_bf.shape[0]
    ec = blocks * N_KEYS
    assert blocks % 8 == 0 and n_exp == N_KEYS * N_KEYS
    tok = pl.BlockSpec((tt, D_MODEL), lambda i, c: (i, 0))
    rows = pl.BlockSpec((PEER_HEADS, blocks, tt), lambda i, c: (0, c, i))
    tiles = pl.BlockSpec((PEER_HEADS, N_KEYS, tt), lambda i, c: (0, 0, i))
    return pl.pallas_call(
        functools.partial(_peer_dense_body, blocks=blocks, final_norm=final_norm),
        grid=(n // tt, n_exp // ec),
        in_specs=[tok, pl.BlockSpec((D_MODEL, tt), lambda i, c: (0, i)), rows, tiles, rows, tiles,
                  pl.BlockSpec((ec, D_MODEL), lambda i, c: (c, 0)),
                  pl.BlockSpec((D_MODEL, ec), lambda i, c: (0, c)),
                  pl.BlockSpec((1, D_MODEL), lambda i, c: (0, 0))],
        out_specs=tok,
        out_shape=jax.ShapeDtypeStruct((n, D_MODEL), F32),
        scratch_shapes=[pltpu.VMEM((D_MODEL, tt), F32), pltpu.VMEM((ec, tt), BF16)],
        compiler_params=_cparams("parallel", "arbitrary"),
        name="peer_dense",
    )(x2d, xnt, r1, n2, e1, e2, u_bf, vt_bf, final_w.reshape(1, D_MODEL))


def _peer(x2d, norm_w, wqt_bf, keys_bf, u_bf, vt_bf, final_w, final_norm, t_sel, t_dense, blocks):
    xnt, r1, n2, e1, e2 = _peer_select(x2d, norm_w, wqt_bf, keys_bf, t_sel)
    return _peer_dense(x2d, xnt, r1, n2, e1, e2, u_bf, vt_bf, final_w, t_dense, blocks, final_norm)


def _tile(n, want):
    return min(n, want)


def kernel(x_prompt, x_sample, cache_attn_k, cache_attn_v, cache_mem_k, cache_mem_v, state_hgrn, page_table, mem_prompt, norm_mix_w, w_in, lambda_q1, lambda_k1, lambda_q2, lambda_k2, diff_ln_w, hgrn_lower_bounds, hgrn_norm_w, w_out, norm_mem_q_w, norm_mem_kv_w, w_mq, w_mk, w_mv, w_mo, norm_ffn_w, peer_wq, peer_keys, peer_u, peer_v, final_norm_w):
    b, l = x_prompt.shape[:2]
    db, ls = x_sample.shape[:2]
    depth = w_in.shape[0]
    past = page_table.shape[1] * PAGE_SIZE
    pos_p = jnp.arange(l)
    tm_s = _tile(db * ls, 256)
    pos_s = past + (jnp.arange(tm_s) % ls)
    xp = x_prompt.reshape(b * l, D_MODEL)
    xs = x_sample.reshape(db * ls, D_MODEL)
    chunk_p = HGRN_CHUNK if l % HGRN_CHUNK == 0 else l
    chunk_s = HGRN_CHUNK if ls % HGRN_CHUNK == 0 else ls
    tq = _tile(l, 512)
    outs = [[] for _ in range(8)]
    for layer in range(depth):
        lam_init = 0.8 - 0.6 * math.exp(-0.3 * layer)
        lam = (jnp.exp(jnp.sum(lambda_q1[layer] * lambda_k1[layer]))
               - jnp.exp(jnp.sum(lambda_q2[layer] * lambda_k2[layer])) + lam_init).reshape(1, 1).astype(F32)
        last = layer == depth - 1
        w_in_bf = w_in[layer].astype(BF16)
        w_out_bf = w_out[layer].astype(BF16)
        wmq, wmk, wmv, wmo = (w[layer].astype(BF16) for w in (w_mq, w_mk, w_mv, w_mo))
        wqt_bf = peer_wq[layer].astype(BF16).T
        keys_bf = peer_keys[layer].astype(BF16).reshape(PEER_HEADS * 2, N_KEYS, PEER_HALF)
        u_bf = peer_u[layer].astype(BF16)
        vt_bf = peer_v[layer].astype(BF16).T
        peer = functools.partial(_peer, norm_w=norm_ffn_w[layer], wqt_bf=wqt_bf, keys_bf=keys_bf, u_bf=u_bf,
                                 vt_bf=vt_bf, final_w=final_norm_w, final_norm=last)

        ka, va, qat, kab, vat, qb, kb, ib, lf, gb = _inproj(
            xp, norm_mix_w[layer], w_in_bf, hgrn_lower_bounds, pos_p, layer, tq, True)
        oa = _prompt_attention(lam, qat, kab, vat, diff_ln_w[layer], 1.0 - lam_init, b, l, tq)
        n_chunks = max(1, min(l, 512) // chunk_p)
        ob, sp = _hgrn(qb, kb, ib, lf, gb, jnp.zeros((b, H_B, DK_B, DV_B), F32), hgrn_norm_w[layer],
                       b, l, chunk_p, n_chunks)
        xp = _mix(xp, oa, ob, w_out_bf, _tile(l, 512))
        mk, mv = _mem_kv(mem_prompt.reshape(b * MEM_LEN, D_MODEL), norm_mem_kv_w[layer], wmk, wmv, MEM_LEN)
        tm = _tile(l, 512)
        xp = _mem_attend(xp, norm_mem_q_w[layer], wmq, mk.reshape(b, MEM_LEN, D_MODEL),
                         mv.reshape(b, MEM_LEN, D_MODEL), wmo, tm, l // tm)
        xp = peer(xp, t_sel=_tile(b * l, 256), t_dense=_tile(b * l, 512), blocks=8)
        outs[0].append(ka.reshape(b, l, H_A, 2, DK_A))
        outs[1].append(va.reshape(b, l, H_A, DV_A))
        outs[2].append(sp)
        outs[3].append(mk.reshape(b, MEM_LEN, H_M, DH_M))
        outs[4].append(mv.reshape(b, MEM_LEN, H_M, DH_M))

        ka, va, qab, kab, vab, qb, kb, ib, lf, gb = _inproj(
            xs, norm_mix_w[layer], w_in_bf, hgrn_lower_bounds, pos_s, layer, tm_s, False)
        oa = _sample_attention(lam, qab, kab, vab, cache_attn_k[layer], cache_attn_v[layer], page_table,
                               diff_ln_w[layer], 1.0 - lam_init, db, ls, min(8, page_table.shape[1]))
        ob, ss = _hgrn(qb, kb, ib, lf, gb, state_hgrn[layer], hgrn_norm_w[layer], db, ls, chunk_s, ls // chunk_s)
        xs = _mix(xs, oa, ob, w_out_bf, tm_s)
        xs = _mem_attend(xs, norm_mem_q_w[layer], wmq, cache_mem_k[layer].reshape(db, MEM_LEN, D_MODEL),
                         cache_mem_v[layer].reshape(db, MEM_LEN, D_MODEL), wmo, ls, 1)
        xs = peer(xs, t_sel=tm_s, t_dense=_tile(db * ls, 512), blocks=8)
        outs[5].append(ka.reshape(db, ls, H_A, 2, DK_A))
        outs[6].append(va.reshape(db, ls, H_A, DV_A))
        outs[7].append(ss)
    y_prompt = xp.reshape(b, l, D_MODEL)
    y_sample = xs.reshape(db, ls, D_MODEL)
    return (y_prompt, y_sample) + tuple(jnp.stack(o) for o in outs)
```

```python
import functools
import math

import jax
import jax.numpy as jnp
from jax import lax
from jax.experimental import pallas as pl
from jax.experimental.pallas import tpu as pltpu

F32 = jnp.float32
BF16 = jnp.bfloat16
EPS = 1e-6
NEG_INF = float("-inf")

D_MODEL = 1024
PAGE_SIZE = 128
H_A = 4
DV_A = 128
DK_A = 64
ROT_DIM = 16
ROPE_THETA = 500000.0
H_B = 4
DK_B = 128
DV_B = 128
HGRN_CHUNK = 64
HGRN_SUB = 16
MEM_LEN = 256
H_M = 4
DH_M = 256
N_KEYS = 128
PEER_HEADS = 8
PEER_TOPK = 16
PEER_HALF = 128
SEG = 512
N_SEG = 7

VMEM_LIMIT = 56 * 1024 * 1024


def _cparams(*sem, flags=None):
    return pltpu.CompilerParams(dimension_semantics=sem, vmem_limit_bytes=VMEM_LIMIT, flags=flags)


def _rms(x, w):
    return x * lax.rsqrt(jnp.mean(x * x, axis=-1, keepdims=True) + EPS) * w


def _dot_nt(a, b):
    return lax.dot_general(a, b, (((1,), (1,)), ((), ())), preferred_element_type=F32)


def _dot(a, b):
    return jnp.dot(a, b, preferred_element_type=F32)


def _inproj_body(x_ref, nw_ref, w_ref, lbp_ref, rc_ref, rs1_ref, rs2_ref,
                 ka_ref, va_ref, qab_ref, kab_ref, vab_ref,
                 qb_ref, kb_ref, ib_ref, lf_ref, gb_ref, *, layer, transposed):
    hb = _rms(x_ref[...], nw_ref[...]).astype(BF16)

    def seg(i):
        return _dot(hb, w_ref[:, i * SEG:(i + 1) * SEG])

    rc, rs1, rs2 = rc_ref[...], rs1_ref[...], rs2_ref[...]

    def rope_block(blk):
        return blk * rc + pltpu.roll(blk, 128 - ROT_DIM // 2, 1) * rs1 + pltpu.roll(blk, ROT_DIM // 2, 1) * rs2

    qa = seg(0)
    ka = seg(1)
    for g in range(SEG // 128):
        sl = slice(g * 128, (g + 1) * 128)
        qr = rope_block(qa[:, sl]) * (DK_A ** -0.5)
        if transposed:
            qab_ref[sl, :] = qr.T.astype(BF16)
        else:
            qab_ref[:, sl] = qr.astype(BF16)
        kr = rope_block(ka[:, sl])
        if transposed:
            ka_ref[sl, :] = kr.T
        else:
            ka_ref[:, sl] = kr
        kab_ref[:, sl] = kr.astype(BF16)
    va = seg(2)
    va_ref[...] = va
    if transposed:
        for g in range(SEG // 128):
            sl = slice(g * 128, (g + 1) * 128)
            vab_ref[sl, :] = va[:, sl].T.astype(BF16)
    else:
        vab_ref[...] = va.astype(BF16)
    qb_ref[...] = seg(3)
    lbp = lbp_ref[...]
    e = jnp.exp(lbp - jnp.max(lbp, axis=0, keepdims=True))
    lb = jnp.sum(e[:layer + 1], axis=0, keepdims=True) / jnp.sum(e, axis=0, keepdims=True)
    fg = lb + (1.0 - lb) * jax.nn.sigmoid(seg(4))
    lf_ref[...] = jnp.log(fg)
    kb_ref[...] = 1.0 - fg
    ib_ref[...] = seg(5).astype(BF16)
    gb_ref[...] = seg(6)


def _rope_tables(pos):
    half = ROT_DIM // 2
    inv = ROPE_THETA ** (-jnp.arange(half, dtype=F32) * 2.0 / ROT_DIM)
    ang = pos.astype(F32)[:, None] * inv[None, :]
    cos, sin = jnp.cos(ang), jnp.sin(ang)
    p = pos.shape[0]
    ones = jnp.ones((p, DK_A - ROT_DIM), F32)
    zeros = jnp.zeros((p, DK_A - ROT_DIM), F32)
    zh = jnp.zeros((p, half), F32)
    rc = jnp.concatenate([cos, cos, ones], axis=1)
    rs1 = jnp.concatenate([-sin, zh, zeros], axis=1)
    rs2 = jnp.concatenate([zh, sin, zeros], axis=1)
    return tuple(jnp.concatenate([t, t], axis=1) for t in (rc, rs1, rs2))


def _inproj(x2d, norm_w, w_in_bf, lower_bounds, pos, layer, tm, transposed):
    n = x2d.shape[0]
    p = pos.shape[0]
    assert n % tm == 0 and p % tm == 0
    npb = p // tm
    rc, rs1, rs2 = _rope_tables(pos)
    row = lambda i: (i, 0)
    const = lambda i: (0, 0)
    tab = lambda i: (i % npb, 0)
    f32o = jax.ShapeDtypeStruct((n, SEG), F32)
    bf16o = jax.ShapeDtypeStruct((n, SEG), BF16)
    out_spec = pl.BlockSpec((tm, SEG), row)
    out_specs = [out_spec] * 10
    out_shape = [f32o, f32o, bf16o, bf16o, bf16o, f32o, f32o, bf16o, f32o, f32o]
    if transposed:
        t_spec = pl.BlockSpec((None, SEG, tm), lambda i: (i, 0, 0))
        t_shape = jax.ShapeDtypeStruct((n // tm, SEG, tm), BF16)
        out_specs[2] = out_specs[4] = t_spec
        out_shape[2] = out_shape[4] = t_shape
        out_specs[0] = pl.BlockSpec((None, SEG, tm), lambda i: (i // npb, 0, i % npb))
        out_shape[0] = jax.ShapeDtypeStruct((n // p, SEG, p), F32)
    return pl.pallas_call(
        functools.partial(_inproj_body, layer=layer, transposed=transposed),
        grid=(n // tm,),
        in_specs=[
            pl.BlockSpec((tm, D_MODEL), row),
            pl.BlockSpec((1, D_MODEL), const),
            pl.BlockSpec((D_MODEL, N_SEG * SEG), const),
            pl.BlockSpec(lower_bounds.shape, const),
            pl.BlockSpec((tm, 128), tab),
            pl.BlockSpec((tm, 128), tab),
            pl.BlockSpec((tm, 128), tab),
        ],
        out_specs=out_specs,
        out_shape=out_shape,
        compiler_params=_cparams("parallel"),
        name="inproj",
    )(x2d, norm_w.reshape(1, D_MODEL), w_in_bf, lower_bounds, rc, rs1, rs2)


def _split_maps(q):
    lane = lax.broadcasted_iota(jnp.int32, q.shape, 1)
    zero = jnp.zeros_like(q)
    return jnp.concatenate([jnp.where(lane < DK_A, q, zero), jnp.where(lane >= DK_A, q, zero)], axis=0)


def _online_update(s, v, m_ref, l_ref, acc_ref):
    m_prev = m_ref[...]
    m_new = jnp.maximum(m_prev, jnp.max(s, axis=-1, keepdims=True))
    alpha = jnp.exp(m_prev - m_new)
    p = jnp.exp(s - m_new)
    l_ref[...] = alpha * l_ref[...] + jnp.sum(p, axis=-1, keepdims=True)
    acc_ref[...] = alpha * acc_ref[...] + _dot(p.astype(BF16), v)
    m_ref[...] = m_new


def _diff_finish(acc, l, t, lam, lnw, out_scale):
    o = acc[:t] / l[:t] - lam * (acc[t:] / l[t:])
    return _rms(o, lnw) * out_scale


def _pattn_body(lam_ref, qt_ref, k_ref, vt_ref, lnw_ref, o_ref, m_ref, l_ref, acc_ref, *, t, r, out_scale):
    i = pl.program_id(2)
    qt = jnp.concatenate([qt_ref[c] for c in range(r)], axis=1)
    feat = lax.broadcasted_iota(jnp.int32, qt.shape, 0)
    zero = jnp.zeros_like(qt)
    q2t = jnp.concatenate([jnp.where(feat < DK_A, qt, zero), jnp.where(feat >= DK_A, qt, zero)], axis=1)
    m_ref[...] = jnp.full(m_ref.shape, NEG_INF, F32)
    l_ref[...] = jnp.zeros(l_ref.shape, F32)
    acc_ref[...] = jnp.zeros(acc_ref.shape, F32)

    def block(j, masked):
        k_blk = jnp.concatenate([k_ref[j * r + c] for c in range(r)], axis=0)
        vt_blk = jnp.concatenate([vt_ref[j * r + c] for c in range(r)], axis=1)
        s = _dot(k_blk, q2t)
        if masked:
            key = lax.broadcasted_iota(jnp.int32, s.shape, 0) + j * t
            qp = lax.broadcasted_iota(jnp.int32, s.shape, 1)
            qp = jnp.where(qp >= t, qp - t, qp) + i * t
            s = jnp.where(key <= qp, s, NEG_INF)
        m_prev = m_ref[...]
        m_new = jnp.maximum(m_prev, jnp.max(s, axis=0, keepdims=True))
        alpha = jnp.exp(m_prev - m_new)
        p = jnp.exp(s - m_new)
        l_ref[...] = alpha * l_ref[...] + jnp.sum(p, axis=0, keepdims=True)
        acc_ref[...] = alpha * acc_ref[...] + _dot(vt_blk, p.astype(BF16))
        m_ref[...] = m_new

    def full_block(j, carry):
        block(j, False)
        return carry

    lax.fori_loop(0, i, full_block, 0)
    block(i, True)
    acc = acc_ref[...]
    l = l_ref[...]
    ot = acc[:, :t] / l[:, :t] - lam_ref[...] * (acc[:, t:] / l[:, t:])
    ot = ot * lax.rsqrt(jnp.mean(ot * ot, axis=0, keepdims=True) + EPS)
    o_ref[...] = ot.T * lnw_ref[...] * out_scale


def _prompt_attention(lam, qt, kab, vt, diff_ln_w, out_scale, b, l, t):
    tile = qt.shape[2]
    assert l % t == 0 and t % tile == 0
    nq = l // t
    r = t // tile
    nt = l // tile
    k3 = kab.reshape(b * nt, tile, SEG)
    out = pl.pallas_call(
        functools.partial(_pattn_body, t=t, r=r, out_scale=out_scale),
        grid=(b, H_A, nq),
        in_specs=[
            pl.BlockSpec((1, 1), lambda bi, h, i: (0, 0)),
            pl.BlockSpec((r, 128, tile), lambda bi, h, i: (bi * nq + i, h, 0)),
            pl.BlockSpec((nt, tile, 128), lambda bi, h, i: (bi, 0, h)),
            pl.BlockSpec((nt, 128, tile), lambda bi, h, i: (bi, h, 0)),
            pl.BlockSpec((1, DV_A), lambda bi, h, i: (0, 0)),
        ],
        out_specs=pl.BlockSpec((None, t, 128), lambda bi, h, i: (bi, i, h)),
        out_shape=jax.ShapeDtypeStruct((b, l, SEG), F32),
        scratch_shapes=[pltpu.VMEM((1, 2 * t), F32), pltpu.VMEM((1, 2 * t), F32), pltpu.VMEM((128, 2 * t), F32)],
        compiler_params=_cparams("parallel", "parallel", "arbitrary"),
        name="prompt_attn",
    )(lam, qt, k3, vt, diff_ln_w.reshape(1, DV_A))
    return out.reshape(b * l, SEG)


def _sattn_body(pt_ref, lam_ref, q_ref, *refs, n_grp, ls, out_scale):
    k_refs = refs[:n_grp]
    v_refs = refs[n_grp:2 * n_grp]
    kn_ref, vn_ref, lnw_ref, o_ref, q2_ref, m_ref, l_ref, acc_ref = refs[2 * n_grp:]
    j = pl.program_id(1)
    rows = 2 * ls

    @pl.when(j == 0)
    def _():
        q = q_ref[...]
        for h in range(H_A):
            q2_ref[h * rows:(h + 1) * rows, :] = _split_maps(q[:, h * 128:(h + 1) * 128])
        m_ref[...] = jnp.full(m_ref.shape, NEG_INF, F32)
        l_ref[...] = jnp.zeros(l_ref.shape, F32)
        acc_ref[...] = jnp.zeros(acc_ref.shape, F32)

    def head_update(h, kt, v, mask):
        hs = slice(h * rows, (h + 1) * rows)
        s = _dot(q2_ref[hs, :], kt)
        if mask is not None:
            s = jnp.where(mask, s, NEG_INF)
        _online_update(s, v, m_ref.at[hs, :], l_ref.at[hs, :], acc_ref.at[hs, :])

    for h in range(H_A):
        hl = slice(h * 128, (h + 1) * 128)
        head_update(h, jnp.concatenate([r[hl, :].astype(BF16) for r in k_refs], axis=1),
                    jnp.concatenate([r[pl.ds(h, PAGE_SIZE, stride=H_A), :].astype(BF16) for r in v_refs], axis=0),
                    None)

    @pl.when(j == pl.num_programs(1) - 1)
    def _():
        knt = kn_ref[...]
        vn = vn_ref[...]
        r = lax.broadcasted_iota(jnp.int32, (rows, PAGE_SIZE), 0)
        r = jnp.where(r >= ls, r - ls, r)
        c = lax.broadcasted_iota(jnp.int32, (rows, PAGE_SIZE), 1)
        mask = c <= r
        for h in range(H_A):
            hl = slice(h * 128, (h + 1) * 128)
            hs = slice(h * rows, (h + 1) * rows)
            head_update(h, knt[hl, :], vn[:, hl], mask)
            o_ref[:, hl] = _diff_finish(acc_ref[hs, :], l_ref[hs, :], ls, lam_ref[...], lnw_ref[...], out_scale)


def _sample_attention(lam, qab, kab, vab, cache_k, cache_v, page_table, diff_ln_w, out_scale, db, ls, n_grp):
    n_pages = page_table.shape[1]
    assert n_pages % n_grp == 0
    n_pool = cache_k.shape[0]
    ck = jnp.transpose(cache_k, (0, 2, 3, 4, 1)).reshape(n_pool, SEG, PAGE_SIZE)
    cv = cache_v.reshape(n_pool, PAGE_SIZE * H_A, DV_A)
    q3 = qab.reshape(db, ls, SEG)
    kn = jnp.pad(jnp.transpose(kab.reshape(db, ls, SEG), (0, 2, 1)), ((0, 0), (0, 0), (0, PAGE_SIZE - ls)))
    vn = jnp.pad(vab.reshape(db, ls, SEG), ((0, 0), (0, PAGE_SIZE - ls), (0, 0)))

    def page_spec(g, shape):
        return pl.BlockSpec((None,) + shape, lambda bi, j, pt: (pt[bi * n_pages + j * n_grp + g], 0, 0))

    per_b = pl.BlockSpec((None, ls, SEG), lambda bi, j, pt: (bi, 0, 0))
    rows = 2 * ls * H_A
    grid_spec = pltpu.PrefetchScalarGridSpec(
        num_scalar_prefetch=1,
        grid=(db, n_pages // n_grp),
        in_specs=[pl.BlockSpec((1, 1), lambda bi, j, pt: (0, 0)), per_b]
        + [page_spec(g, (SEG, PAGE_SIZE)) for g in range(n_grp)]
        + [page_spec(g, (PAGE_SIZE * H_A, DV_A)) for g in range(n_grp)]
        + [pl.BlockSpec((None, SEG, PAGE_SIZE), lambda bi, j, pt: (bi, 0, 0)),
           pl.BlockSpec((None, PAGE_SIZE, SEG), lambda bi, j, pt: (bi, 0, 0)),
           pl.BlockSpec((1, DV_A), lambda bi, j, pt: (0, 0))],
        out_specs=per_b,
        scratch_shapes=[pltpu.VMEM((rows, 128), BF16), pltpu.VMEM((rows, 1), F32),
                        pltpu.VMEM((rows, 1), F32), pltpu.VMEM((rows, 128), F32)],
    )
    out = pl.pallas_call(
        functools.partial(_sattn_body, n_grp=n_grp, ls=ls, out_scale=out_scale),
        grid_spec=grid_spec,
        out_shape=jax.ShapeDtypeStruct((db, ls, SEG), F32),
        compiler_params=_cparams("parallel", "arbitrary"),
        name="sample_attn",
    )(page_table.reshape(-1), lam, q3, *([ck] * n_grp), *([cv] * n_grp), kn, vn, diff_ln_w.reshape(1, DV_A))
    return out.reshape(db * ls, SEG)


def _hgrn_body(q_ref, k_ref, v_ref, lf_ref, g_ref, s0_ref, nw_ref, o_ref, sout_ref, st_ref, b_ref,
               *, chunk, sub, n_chunks):
    t = pl.program_id(1)

    @pl.when(t == 0)
    def _():
        for h in range(H_B):
            st_ref[h] = s0_ref[h].T

    r = lax.broadcasted_iota(jnp.int32, (chunk, chunk), 0)
    c = lax.broadcasted_iota(jnp.int32, (chunk, chunk), 1)
    tri = (c <= r).astype(F32)
    nw = nw_ref[...]

    n_sub = chunk // sub
    heads = range(H_B)
    hl = [slice(h * DK_B, (h + 1) * DK_B) for h in heads]

    def one_chunk(ci, carry):
        rows = pl.ds(pl.multiple_of(ci * chunk, chunk), chunk)
        b_ref[...] = jnp.dot(tri, lf_ref[rows, :], precision=lax.Precision.HIGHEST, preferred_element_type=F32)
        q = [q_ref[rows, hl[h]] for h in heads]
        k = [k_ref[rows, hl[h]] for h in heads]
        v = [v_ref[rows, hl[h]] for h in heads]
        b = [b_ref[:, hl[h]] for h in heads]
        st = [st_ref[h] for h in heads]
        att = {}
        for h in heads:
            for j in range(n_sub):
                ref = b_ref[j * sub - 1:j * sub, hl[h]] if j > 0 else jnp.zeros((1, DK_B), F32)
                ncol = (j + 1) * sub
                qj = (q[h][j * sub:ncol] * jnp.exp(b[h][j * sub:ncol] - ref)).astype(BF16)
                kj = (k[h][:ncol] * jnp.exp(ref - b[h][:ncol])).astype(BF16)
                att[h, j] = _dot_nt(qj, kj)
        inter = [_dot_nt((q[h] * jnp.exp(b[h])).astype(BF16), st[h].astype(BF16)) for h in heads]
        b_last = [b[h][chunk - 1:chunk, :] for h in heads]
        upd = [_dot(v[h].T, (k[h] * jnp.exp(b_last[h] - b[h])).astype(BF16)) for h in heads]
        for h in heads:
            o_parts = []
            for j in range(n_sub):
                ncol = (j + 1) * sub
                causal = (lax.broadcasted_iota(jnp.int32, (sub, ncol), 1)
                          <= lax.broadcasted_iota(jnp.int32, (sub, ncol), 0) + j * sub)
                a = jnp.where(causal, att[h, j], 0.0)
                o_parts.append(_dot(a.astype(BF16), v[h][:ncol]))
            o = jnp.concatenate(o_parts, axis=0) + inter[h]
            st_ref[h] = st[h] * jnp.exp(b_last[h]) + upd[h]
            gate = g_ref[rows, hl[h]]
            o_ref[rows, hl[h]] = _rms(o, nw) * (gate * jax.nn.sigmoid(gate))
        return carry

    lax.fori_loop(0, n_chunks, one_chunk, 0)

    @pl.when(t == pl.num_programs(1) - 1)
    def _():
        for h in range(H_B):
            sout_ref[h] = st_ref[h].T


def _hgrn(qb, kb, ib, lf, gb, s0, norm_w, b, l, chunk, n_chunks):
    rows = chunk * n_chunks
    assert l % rows == 0
    sub = min(HGRN_SUB, chunk)
    a3 = lambda t: t.reshape(b, l, SEG)
    tok = pl.BlockSpec((None, rows, SEG), lambda bi, t: (bi, t, 0))
    st = pl.BlockSpec((None, H_B, DK_B, DV_B), lambda bi, t: (bi, 0, 0, 0))
    out, s_fin = pl.pallas_call(
        functools.partial(_hgrn_body, chunk=chunk, sub=sub, n_chunks=n_chunks),
        grid=(b, l // rows),
        in_specs=[tok, tok, tok, tok, tok, st, pl.BlockSpec((1, DV_B), lambda bi, t: (0, 0))],
        out_specs=[tok, st],
        out_shape=[jax.ShapeDtypeStruct((b, l, SEG), F32), jax.ShapeDtypeStruct((b, H_B, DK_B, DV_B), F32)],
        scratch_shapes=[pltpu.VMEM((H_B, DV_B, DK_B), F32), pltpu.VMEM((chunk, SEG), F32)],
        compiler_params=_cparams("parallel", "arbitrary"),
        name="hgrn2",
    )(a3(qb), a3(kb), a3(ib), a3(lf), a3(gb), s0, norm_w.reshape(1, DV_B))
    return out.reshape(b * l, SEG), s_fin


def _mix_body(x_ref, oa_ref, ob_ref, w_ref, o_ref):
    o_ref[...] = (x_ref[...] + _dot(oa_ref[...].astype(BF16), w_ref[:SEG, :])
                  + _dot(ob_ref[...].astype(BF16), w_ref[SEG:, :]))


def _mix(x2d, oa, ob, w_out_bf, tm):
    n = x2d.shape[0]
    row = lambda i: (i, 0)
    return pl.pallas_call(
        _mix_body,
        grid=(n // tm,),
        in_specs=[pl.BlockSpec((tm, D_MODEL), row), pl.BlockSpec((tm, SEG), row), pl.BlockSpec((tm, SEG), row),
                  pl.BlockSpec((D_MODEL, D_MODEL), lambda i: (0, 0))],
        out_specs=pl.BlockSpec((tm, D_MODEL), row),
        out_shape=jax.ShapeDtypeStruct((n, D_MODEL), F32),
        compiler_params=_cparams("parallel"),
        name="mix_out",
    )(x2d, oa, ob, w_out_bf)


def _memkv_body(m_ref, nw_ref, wk_ref, wv_ref, k_ref, v_ref):
    hb = _rms(m_ref[...], nw_ref[...]).astype(BF16)
    k_ref[...] = _dot(hb, wk_ref[...])
    v_ref[...] = _dot(hb, wv_ref[...])


def _mem_kv(mem2d, norm_w, wk_bf, wv_bf, tm):
    n = mem2d.shape[0]
    row = lambda i: (i, 0)
    const = lambda i: (0, 0)
    o = jax.ShapeDtypeStruct((n, D_MODEL), F32)
    return pl.pallas_call(
        _memkv_body,
        grid=(n // tm,),
        in_specs=[pl.BlockSpec((tm, D_MODEL), row), pl.BlockSpec((1, D_MODEL), const),
                  pl.BlockSpec((D_MODEL, D_MODEL), const), pl.BlockSpec((D_MODEL, D_MODEL), const)],
        out_specs=[pl.BlockSpec((tm, D_MODEL), row)] * 2,
        out_shape=[o, o],
        compiler_params=_cparams("parallel"),
        name="mem_kv",
    )(mem2d, norm_w.reshape(1, D_MODEL), wk_bf, wv_bf)


def _memattn_body(x_ref, nw_ref, wq_ref, mk_ref, mv_ref, wo_ref, o_ref, *, head_rows):
    x = x_ref[...]
    q = _dot(_rms(x, nw_ref[...]).astype(BF16), wq_ref[...])
    q = (q * (DH_M ** -0.5)).astype(BF16)
    heads = []

    def head_of(ref, h):
        halves = [ref[pl.ds(half * H_M + h, MEM_LEN, stride=2 * H_M), :] for half in range(2)]
        return jnp.concatenate(halves, axis=1).astype(BF16)

    for h in range(H_M):
        hl = slice(h * DH_M, (h + 1) * DH_M)
        if head_rows:
            mk, mv = head_of(mk_ref, h), head_of(mv_ref, h)
        else:
            mk, mv = mk_ref[:, hl].astype(BF16), mv_ref[:, hl].astype(BF16)
        s = _dot_nt(q[:, hl], mk)
        p = jnp.exp(s - jnp.max(s, axis=-1, keepdims=True))
        o = _dot(p.astype(BF16), mv)
        heads.append(o / jnp.sum(p, axis=-1, keepdims=True))
    o_ref[...] = x + _dot(jnp.concatenate(heads, axis=1).astype(BF16), wo_ref[...])


def _mem_attend(x2d, norm_w, wq_bf, mk, mv, wo_bf, tm, tiles_per_batch):
    n = x2d.shape[0]
    row = lambda i: (i, 0)
    const = lambda i: (0, 0)
    head_rows = mk.shape[1] == MEM_LEN * 2 * H_M
    mem = pl.BlockSpec((None,) + mk.shape[1:], lambda i: (i // tiles_per_batch, 0, 0))
    return pl.pallas_call(
        functools.partial(_memattn_body, head_rows=head_rows),
        grid=(n // tm,),
        in_specs=[pl.BlockSpec((tm, D_MODEL), row), pl.BlockSpec((1, D_MODEL), const),
                  pl.BlockSpec((D_MODEL, D_MODEL), const), mem, mem, pl.BlockSpec((D_MODEL, D_MODEL), const)],
        out_specs=pl.BlockSpec((tm, D_MODEL), row),
        out_shape=jax.ShapeDtypeStruct((n, D_MODEL), F32),
        compiler_params=_cparams("parallel"),
        name="mem_attn",
    )(x2d, norm_w.reshape(1, D_MODEL), wq_bf, mk, mv, wo_bf)


_REMOVED_EXP = 100
_REMOVED_BITS = ((_REMOVED_EXP + 127) << 23) - (1 << 31)


def _extract_topk(work_ref, rank_ref, vals_ref, n_rows, k, exact):
    shape = work_ref.shape
    row = lax.broadcasted_iota(jnp.int32, shape, 0)

    def body(a, carry):
        w = work_ref[...]
        m = jnp.max(w, axis=0, keepdims=True)
        if exact:
            hit = row == jnp.min(jnp.where(w == m, row, n_rows), axis=0, keepdims=True)
        else:
            hit = w == m
        marker = pltpu.bitcast(jnp.full(shape, _REMOVED_BITS, jnp.int32) + (a << 23), F32)
        work_ref[...] = jnp.where(hit, marker, w)
        vals_ref[pl.ds(a, 1), :] = m
        return carry

    lax.fori_loop(0, k, body, 0)
    bits = pltpu.bitcast(work_ref[...], jnp.int32)
    order = ((bits >> 23) & 0xFF) - (_REMOVED_EXP + 127)
    removed = work_ref[...] <= -(2.0 ** _REMOVED_EXP)
    rank = jnp.where(removed, order, k).astype(F32)
    rank_ref[...] = rank
    n_removed = jnp.sum((rank < float(k)).astype(F32), axis=0, keepdims=True)
    return (n_removed > float(k)).astype(F32)


def _peer_select_body(x_ref, nw_ref, wq_ref, keys_ref, xn_ref, r1_ref, n2_ref, e1_ref, e2_ref, qt_ref,
                      work_ref, rank1_ref, rank2_ref, vals1_ref, vals2_ref, cand_ref, crank_ref, cvals_ref):
    hbt = _rms(x_ref[...], nw_ref[...]).T.astype(BF16)
    xn_ref[...] = hbt
    qt_ref[...] = _dot(wq_ref[...], hbt).astype(BF16)
    k = PEER_TOPK

    def select(exact):
        tie = jnp.zeros((1, qt_ref.shape[1]), F32)
        for h in range(PEER_HEADS):
            e_half = []
            for p, (rank_ref, vals_ref) in enumerate(((rank1_ref, vals1_ref), (rank2_ref, vals2_ref))):
                hp = 2 * h + p
                s = _dot(keys_ref[hp], qt_ref[hp * PEER_HALF:(hp + 1) * PEER_HALF, :])
                e_half.append(jnp.exp(s - jnp.max(s, axis=0, keepdims=True)))
                work_ref[...] = s
                tie = jnp.maximum(tie, _extract_topk(work_ref, rank_ref, vals_ref, N_KEYS, k, exact))
            v1 = vals1_ref[...]
            v2 = vals2_ref[...]
            cand_ref[...] = jnp.full(cand_ref.shape, NEG_INF, F32)
            for a, (off, nb_a) in enumerate(_CAND_ROWS):
                cand_ref[off:off + nb_a, :] = v1[a:a + 1, :] + v2[:nb_a, :]
            tie = jnp.maximum(tie, _extract_topk(cand_ref, crank_ref, cvals_ref, _N_CAND_PAD, k, exact))
            cv = cvals_ref[...]
            z = jnp.sum(jnp.exp(cv - cv[0:1, :]), axis=0, keepdims=True)
            sel = (crank_ref[...] < float(k)).astype(F32)
            tt = sel.shape[1]
            nb = None
            for off, nb_a in _CAND_ROWS:
                rows = sel[off:off + nb_a, :]
                if nb_a < k:
                    rows = jnp.concatenate([rows, jnp.zeros((k - nb_a, tt), F32)], axis=0)
                nb = rows if nb is None else nb + rows
            rank2 = rank2_ref[...]
            n2 = jnp.zeros(rank2.shape, F32)
            for b in range(k):
                n2 = jnp.where(rank2 == float(b), nb[b:b + 1, :], n2)
            r1_ref[h] = rank1_ref[...]
            n2_ref[h] = n2.astype(BF16)
            e1_ref[h] = e_half[0]
            e2_ref[h] = (e_half[1] / z).astype(BF16)
        return tie

    tie = select(False)

    @pl.when(jnp.max(tie) > 0.0)
    def _():
        select(True)


def _cand_rows(k):
    rows, off = [], 0
    for a in range(k):
        nb_a = k // (a + 1)
        rows.append((off, nb_a))
        off += nb_a
    return tuple(rows), off


_CAND_ROWS, _N_CAND = _cand_rows(PEER_TOPK)
_N_CAND_PAD = -(-_N_CAND // 8) * 8


def _peer_select(x2d, norm_w, wq_bf, keys_bf, tt):
    n = x2d.shape[0]
    k = PEER_TOPK
    const2 = lambda i: (0, 0)
    tokmajor = pl.BlockSpec((PEER_HEADS, N_KEYS, tt), lambda i: (0, 0, i))
    kt = jax.ShapeDtypeStruct((PEER_HEADS, N_KEYS, n), F32)
    kt16 = jax.ShapeDtypeStruct((PEER_HEADS, N_KEYS, n), BF16)
    return pl.pallas_call(
        _peer_select_body,
        grid=(n // tt,),
        in_specs=[pl.BlockSpec((tt, D_MODEL), lambda i: (i, 0)), pl.BlockSpec((1, D_MODEL), const2),
                  pl.BlockSpec(wq_bf.shape, const2), pl.BlockSpec(keys_bf.shape, lambda i: (0, 0, 0))],
        out_specs=[pl.BlockSpec((D_MODEL, tt), lambda i: (0, i)), tokmajor, tokmajor, tokmajor, tokmajor],
        out_shape=[jax.ShapeDtypeStruct((D_MODEL, n), BF16), kt, kt16, kt, kt16],
        scratch_shapes=[pltpu.VMEM((2 * PEER_HEADS * PEER_HALF, tt), BF16),
                        pltpu.VMEM((N_KEYS, tt), F32), pltpu.VMEM((N_KEYS, tt), F32), pltpu.VMEM((N_KEYS, tt), F32),
                        pltpu.VMEM((k, tt), F32), pltpu.VMEM((k, tt), F32),
                        pltpu.VMEM((_N_CAND_PAD, tt), F32), pltpu.VMEM((_N_CAND_PAD, tt), F32),
                        pltpu.VMEM((k, tt), F32)],
        compiler_params=_cparams("parallel"),
        name="peer_select",
    )(x2d, norm_w.reshape(1, D_MODEL), wq_bf, keys_bf)


def _peer_dense_body(x_ref, xnt_ref, r1_ref, n2_ref, e1_ref, e2_ref, u_ref, vt_ref, fw_ref, o_ref,
                     yt_ref, ht_ref, *, blocks, final_norm):
    c = pl.program_id(1)

    @pl.when(c == 0)
    def _():
        yt_ref[...] = jnp.zeros(yt_ref.shape, F32)

    tt = xnt_ref.shape[1]
    pre = _dot(u_ref[...], xnt_ref[...])
    act = (0.5 * pre * (1.0 + lax.erf(pre * (2.0 ** -0.5)))).astype(BF16)
    reps = N_KEYS // _ROW_BCAST

    def bcast_row(ref, h, g):
        row = jnp.broadcast_to(ref[h, g:g + 1, :], (_ROW_BCAST, tt)).astype(BF16)
        return jnp.concatenate([row] * reps, axis=0)

    for g in range(blocks):
        w = None
        for h in range(PEER_HEADS):
            r1 = bcast_row(r1_ref, h, g)
            gate = bcast_row(e1_ref, h, g) * e2_ref[h]
            term = jnp.where(r1 < n2_ref[h], gate, jnp.zeros_like(gate))
            w = term if w is None else w + term
        ht_ref[g * N_KEYS:(g + 1) * N_KEYS, :] = w * act[g * N_KEYS:(g + 1) * N_KEYS, :]
    yt_ref[...] += _dot(vt_ref[...], ht_ref[...])

    @pl.when(c == pl.num_programs(1) - 1)
    def _():
        y = x_ref[...] + yt_ref[...].T
        o_ref[...] = _rms(y, fw_ref[...]) if final_norm else y


_ROW_BCAST = 16


def _peer_dense(x2d, xnt, r1, n2, e1, e2, u_bf, vt_bf, final_w, tt, blocks, final_norm):
    n = x2d.shape[0]
    n_exp = u_bf.shape[0]
    ec = blocks * N_KEYS
    assert blocks % 8 == 0 and n_exp == N_KEYS * N_KEYS
    tok = pl.BlockSpec((tt, D_MODEL), lambda i, c: (i, 0))
    rows = pl.BlockSpec((PEER_HEADS, blocks, tt), lambda i, c: (0, c, i))
    tiles = pl.BlockSpec((PEER_HEADS, N_KEYS, tt), lambda i, c: (0, 0, i))
    return pl.pallas_call(
        functools.partial(_peer_dense_body, blocks=blocks, final_norm=final_norm),
        grid=(n // tt, n_exp // ec),
        in_specs=[tok, pl.BlockSpec((D_MODEL, tt), lambda i, c: (0, i)), rows, tiles, rows, tiles,
                  pl.BlockSpec((ec, D_MODEL), lambda i, c: (c, 0)),
                  pl.BlockSpec((D_MODEL, ec), lambda i, c: (0, c)),
                  pl.BlockSpec((1, D_MODEL), lambda i, c: (0, 0))],
        out_specs=tok,
        out_shape=jax.ShapeDtypeStruct((n, D_MODEL), F32),
        scratch_shapes=[pltpu.VMEM((D_MODEL, tt), F32), pltpu.VMEM((ec, tt), BF16)],
        compiler_params=_cparams("parallel", "arbitrary"),
        name="peer_dense",
    )(x2d, xnt, r1, n2, e1, e2, u_bf, vt_bf, final_w.reshape(1, D_MODEL))


def _peer(x2d, norm_w, wqt_bf, keys_bf, u_bf, vt_bf, final_w, final_norm, t_sel, t_dense, blocks):
    xnt, r1, n2, e1, e2 = _peer_select(x2d, norm_w, wqt_bf, keys_bf, t_sel)
    return _peer_dense(x2d, xnt, r1, n2, e1, e2, u_bf, vt_bf, final_w, t_dense, blocks, final_norm)


def _tile(n, want):
    return min(n, want)


def kernel(x_prompt, x_sample, cache_attn_k, cache_attn_v, cache_mem_k, cache_mem_v, state_hgrn, page_table, mem_prompt, norm_mix_w, w_in, lambda_q1, lambda_k1, lambda_q2, lambda_k2, diff_ln_w, hgrn_lower_bounds, hgrn_norm_w, w_out, norm_mem_q_w, norm_mem_kv_w, w_mq, w_mk, w_mv, w_mo, norm_ffn_w, peer_wq, peer_keys, peer_u, peer_v, final_norm_w):
    b, l = x_prompt.shape[:2]
    db, ls = x_sample.shape[:2]
    depth = w_in.shape[0]
    past = page_table.shape[1] * PAGE_SIZE
    pos_p = jnp.arange(l)
    tm_s = _tile(db * ls, 256)
    pos_s = past + (jnp.arange(tm_s) % ls)
    xp = x_prompt.reshape(b * l, D_MODEL)
    xs = x_sample.reshape(db * ls, D_MODEL)
    chunk_p = HGRN_CHUNK if l % HGRN_CHUNK == 0 else l
    chunk_s = HGRN_CHUNK if ls % HGRN_CHUNK == 0 else ls
    tq = _tile(l, 512)
    outs = [[] for _ in range(8)]
    for layer in range(depth):
        lam_init = 0.8 - 0.6 * math.exp(-0.3 * layer)
        lam = (jnp.exp(jnp.sum(lambda_q1[layer] * lambda_k1[layer]))
               - jnp.exp(jnp.sum(lambda_q2[layer] * lambda_k2[layer])) + lam_init).reshape(1, 1).astype(F32)
        last = layer == depth - 1
        w_in_bf = w_in[layer].astype(BF16)
        w_out_bf = w_out[layer].astype(BF16)
        wmq, wmk, wmv, wmo = (w[layer].astype(BF16) for w in (w_mq, w_mk, w_mv, w_mo))
        wqt_bf = peer_wq[layer].astype(BF16).T
        keys_bf = peer_keys[layer].astype(BF16).reshape(PEER_HEADS * 2, N_KEYS, PEER_HALF)
        u_bf = peer_u[layer].astype(BF16)
        vt_bf = peer_v[layer].astype(BF16).T
        peer = functools.partial(_peer, norm_w=norm_ffn_w[layer], wqt_bf=wqt_bf, keys_bf=keys_bf, u_bf=u_bf,
                                 vt_bf=vt_bf, final_w=final_norm_w, final_norm=last)

        kat, va, qat, kab, vat, qb, kb, ib, lf, gb = _inproj(
            xp, norm_mix_w[layer], w_in_bf, hgrn_lower_bounds, pos_p, layer, tq, True)
        oa = _prompt_attention(lam, qat, kab, vat, diff_ln_w[layer], 1.0 - lam_init, b, l, _tile(l, 1024))
        n_chunks = max(1, min(l, 512) // chunk_p)
        ob, sp = _hgrn(qb, kb, ib, lf, gb, jnp.zeros((b, H_B, DK_B, DV_B), F32), hgrn_norm_w[layer],
                       b, l, chunk_p, n_chunks)
        xp = _mix(xp, oa, ob, w_out_bf, _tile(l, 512))
        mk, mv = _mem_kv(mem_prompt.reshape(b * MEM_LEN, D_MODEL), norm_mem_kv_w[layer], wmk, wmv, MEM_LEN)
        tm = _tile(l, 512)
        xp = _mem_attend(xp, norm_mem_q_w[layer], wmq, mk.reshape(b, MEM_LEN, D_MODEL),
                         mv.reshape(b, MEM_LEN, D_MODEL), wmo, tm, l // tm)
        xp = peer(xp, t_sel=_tile(b * l, 512), t_dense=_tile(b * l, 512), blocks=8)
        outs[0].append(jnp.transpose(kat.reshape(b, H_A, 2, DK_A, l), (0, 4, 1, 2, 3)))
        outs[1].append(va.reshape(b, l, H_A, DV_A))
        outs[2].append(sp)
        outs[3].append(mk.reshape(b, MEM_LEN, H_M, DH_M))
        outs[4].append(mv.reshape(b, MEM_LEN, H_M, DH_M))

        ka, va, qab, kab, vab, qb, kb, ib, lf, gb = _inproj(
            xs, norm_mix_w[layer], w_in_bf, hgrn_lower_bounds, pos_s, layer, tm_s, False)
        oa = _sample_attention(lam, qab, kab, vab, cache_attn_k[layer], cache_attn_v[layer], page_table,
                               diff_ln_w[layer], 1.0 - lam_init, db, ls, min(8, page_table.shape[1]))
        ob, ss = _hgrn(qb, kb, ib, lf, gb, state_hgrn[layer], hgrn_norm_w[layer], db, ls, chunk_s, ls // chunk_s)
        xs = _mix(xs, oa, ob, w_out_bf, tm_s)
        mem_view = lambda c: jnp.transpose(c.reshape(db, MEM_LEN, H_M, 2, DH_M // 2),
                                           (0, 1, 3, 2, 4)).reshape(db, MEM_LEN * 2 * H_M, DH_M // 2)
        xs = _mem_attend(xs, norm_mem_q_w[layer], wmq, mem_view(cache_mem_k[layer]),
                         mem_view(cache_mem_v[layer]), wmo, ls, 1)
        xs = peer(xs, t_sel=_tile(db * ls, 512), t_dense=_tile(db * ls, 512), blocks=8)
        outs[5].append(ka.reshape(db, ls, H_A, 2, DK_A))
        outs[6].append(va.reshape(db, ls, H_A, DV_A))
        outs[7].append(ss)
    y_prompt = xp.reshape(b, l, D_MODEL)
    y_sample = xs.reshape(db, ls, D_MODEL)
    return (y_prompt, y_sample) + tuple(jnp.stack(o) for o in outs)
```

```python
import functools
import math

import jax
import jax.numpy as jnp
from jax import lax
from jax.experimental import pallas as pl
from jax.experimental.pallas import tpu as pltpu

F32 = jnp.float32
BF16 = jnp.bfloat16
EPS = 1e-6
NEG_INF = float("-inf")

D_MODEL = 1024
PAGE_SIZE = 128
H_A = 4
DV_A = 128
DK_A = 64
ROT_DIM = 16
ROPE_THETA = 500000.0
H_B = 4
DK_B = 128
DV_B = 128
HGRN_CHUNK = 64
HGRN_SUB = 16
MEM_LEN = 256
H_M = 4
DH_M = 256
N_KEYS = 128
PEER_HEADS = 8
PEER_TOPK = 16
PEER_HALF = 128
SEG = 512
N_SEG = 7

VMEM_LIMIT = 56 * 1024 * 1024


def _cparams(*sem, flags=None):
    return pltpu.CompilerParams(dimension_semantics=sem, vmem_limit_bytes=VMEM_LIMIT, flags=flags)


def _rms(x, w):
    return x * lax.rsqrt(jnp.mean(x * x, axis=-1, keepdims=True) + EPS) * w


def _dot_nt(a, b):
    return lax.dot_general(a, b, (((1,), (1,)), ((), ())), preferred_element_type=F32)


def _dot(a, b):
    return jnp.dot(a, b, preferred_element_type=F32)


def _inproj_body(x_ref, nw_ref, w_ref, lbp_ref, rc_ref, rs1_ref, rs2_ref,
                 ka_ref, va_ref, qab_ref, kab_ref, vab_ref,
                 qb_ref, kb_ref, ib_ref, lf_ref, gb_ref, *, layer, transposed):
    hb = _rms(x_ref[...], nw_ref[...]).astype(BF16)

    def seg(i):
        return _dot(hb, w_ref[:, i * SEG:(i + 1) * SEG])

    rc, rs1, rs2 = rc_ref[...], rs1_ref[...], rs2_ref[...]

    def rope_block(blk):
        return blk * rc + pltpu.roll(blk, 128 - ROT_DIM // 2, 1) * rs1 + pltpu.roll(blk, ROT_DIM // 2, 1) * rs2

    qa = seg(0)
    ka = seg(1)
    for g in range(SEG // 128):
        sl = slice(g * 128, (g + 1) * 128)
        qr = rope_block(qa[:, sl]) * (DK_A ** -0.5)
        if transposed:
            qab_ref[sl, :] = qr.T.astype(BF16)
        else:
            qab_ref[:, sl] = qr.astype(BF16)
        kr = rope_block(ka[:, sl])
        if transposed:
            ka_ref[sl, :] = kr.T
        else:
            ka_ref[:, sl] = kr
        kab_ref[:, sl] = kr.astype(BF16)
    va = seg(2)
    va_ref[...] = va
    if transposed:
        for g in range(SEG // 128):
            sl = slice(g * 128, (g + 1) * 128)
            vab_ref[sl, :] = va[:, sl].T.astype(BF16)
    else:
        vab_ref[...] = va.astype(BF16)
    qb_ref[...] = seg(3)
    lbp = lbp_ref[...]
    e = jnp.exp(lbp - jnp.max(lbp, axis=0, keepdims=True))
    lb = jnp.sum(e[:layer + 1], axis=0, keepdims=True) / jnp.sum(e, axis=0, keepdims=True)
    fg = lb + (1.0 - lb) * jax.nn.sigmoid(seg(4))
    lf_ref[...] = jnp.log(fg)
    kb_ref[...] = 1.0 - fg
    ib_ref[...] = seg(5).astype(BF16)
    gb_ref[...] = seg(6)


def _rope_tables(pos):
    half = ROT_DIM // 2
    inv = ROPE_THETA ** (-jnp.arange(half, dtype=F32) * 2.0 / ROT_DIM)
    ang = pos.astype(F32)[:, None] * inv[None, :]
    cos, sin = jnp.cos(ang), jnp.sin(ang)
    p = pos.shape[0]
    ones = jnp.ones((p, DK_A - ROT_DIM), F32)
    zeros = jnp.zeros((p, DK_A - ROT_DIM), F32)
    zh = jnp.zeros((p, half), F32)
    rc = jnp.concatenate([cos, cos, ones], axis=1)
    rs1 = jnp.concatenate([-sin, zh, zeros], axis=1)
    rs2 = jnp.concatenate([zh, sin, zeros], axis=1)
    return tuple(jnp.concatenate([t, t], axis=1) for t in (rc, rs1, rs2))


def _inproj(x2d, norm_w, w_in_bf, lower_bounds, pos, layer, tm, transposed):
    n = x2d.shape[0]
    p = pos.shape[0]
    assert n % tm == 0 and p % tm == 0
    npb = p // tm
    rc, rs1, rs2 = _rope_tables(pos)
    row = lambda i: (i, 0)
    const = lambda i: (0, 0)
    tab = lambda i: (i % npb, 0)
    f32o = jax.ShapeDtypeStruct((n, SEG), F32)
    bf16o = jax.ShapeDtypeStruct((n, SEG), BF16)
    out_spec = pl.BlockSpec((tm, SEG), row)
    out_specs = [out_spec] * 10
    out_shape = [f32o, f32o, bf16o, bf16o, bf16o, f32o, f32o, bf16o, f32o, f32o]
    if transposed:
        t_spec = pl.BlockSpec((None, SEG, tm), lambda i: (i, 0, 0))
        t_shape = jax.ShapeDtypeStruct((n // tm, SEG, tm), BF16)
        out_specs[2] = out_specs[4] = t_spec
        out_shape[2] = out_shape[4] = t_shape
        out_specs[0] = pl.BlockSpec((None, SEG, tm), lambda i: (i // npb, 0, i % npb))
        out_shape[0] = jax.ShapeDtypeStruct((n // p, SEG, p), F32)
    return pl.pallas_call(
        functools.partial(_inproj_body, layer=layer, transposed=transposed),
        grid=(n // tm,),
        in_specs=[
            pl.BlockSpec((tm, D_MODEL), row),
            pl.BlockSpec((1, D_MODEL), const),
            pl.BlockSpec((D_MODEL, N_SEG * SEG), const),
            pl.BlockSpec(lower_bounds.shape, const),
            pl.BlockSpec((tm, 128), tab),
            pl.BlockSpec((tm, 128), tab),
            pl.BlockSpec((tm, 128), tab),
        ],
        out_specs=out_specs,
        out_shape=out_shape,
        compiler_params=_cparams("parallel"),
        name="inproj",
    )(x2d, norm_w.reshape(1, D_MODEL), w_in_bf, lower_bounds, rc, rs1, rs2)


def _split_maps(q):
    lane = lax.broadcasted_iota(jnp.int32, q.shape, 1)
    zero = jnp.zeros_like(q)
    return jnp.concatenate([jnp.where(lane < DK_A, q, zero), jnp.where(lane >= DK_A, q, zero)], axis=0)


def _online_update(s, v, m_ref, l_ref, acc_ref):
    m_prev = m_ref[...]
    m_new = jnp.maximum(m_prev, jnp.max(s, axis=-1, keepdims=True))
    alpha = jnp.exp(m_prev - m_new)
    p = jnp.exp(s - m_new)
    l_ref[...] = alpha * l_ref[...] + jnp.sum(p, axis=-1, keepdims=True)
    acc_ref[...] = alpha * acc_ref[...] + _dot(p.astype(BF16), v)
    m_ref[...] = m_new


def _diff_finish(acc, l, t, lam, lnw, out_scale):
    o = acc[:t] / l[:t] - lam * (acc[t:] / l[t:])
    return _rms(o, lnw) * out_scale


def _pattn_body(lam_ref, qt_ref, k_ref, vt_ref, lnw_ref, o_ref, m_ref, l_ref, acc_ref, *, t, r, out_scale):
    i = pl.program_id(2)
    qt = jnp.concatenate([qt_ref[c] for c in range(r)], axis=1)
    feat = lax.broadcasted_iota(jnp.int32, qt.shape, 0)
    zero = jnp.zeros_like(qt)
    q2t = jnp.concatenate([jnp.where(feat < DK_A, qt, zero), jnp.where(feat >= DK_A, qt, zero)], axis=1)
    m_ref[...] = jnp.full(m_ref.shape, NEG_INF, F32)
    l_ref[...] = jnp.zeros(l_ref.shape, F32)
    acc_ref[...] = jnp.zeros(acc_ref.shape, F32)

    def block(j, masked):
        k_blk = jnp.concatenate([k_ref[j * r + c] for c in range(r)], axis=0)
        vt_blk = jnp.concatenate([vt_ref[j * r + c] for c in range(r)], axis=1)
        s = _dot(k_blk, q2t)
        if masked:
            key = lax.broadcasted_iota(jnp.int32, s.shape, 0) + j * t
            qp = lax.broadcasted_iota(jnp.int32, s.shape, 1)
            qp = jnp.where(qp >= t, qp - t, qp) + i * t
            s = jnp.where(key <= qp, s, NEG_INF)
        m_prev = m_ref[...]
        m_new = jnp.maximum(m_prev, jnp.max(s, axis=0, keepdims=True))
        alpha = jnp.exp(m_prev - m_new)
        p = jnp.exp(s - m_new)
        l_ref[...] = alpha * l_ref[...] + jnp.sum(p, axis=0, keepdims=True)
        acc_ref[...] = alpha * acc_ref[...] + _dot(vt_blk, p.astype(BF16))
        m_ref[...] = m_new

    def full_block(j, carry):
        block(j, False)
        return carry

    lax.fori_loop(0, i, full_block, 0)
    block(i, True)
    acc = acc_ref[...]
    l = l_ref[...]
    ot = acc[:, :t] / l[:, :t] - lam_ref[...] * (acc[:, t:] / l[:, t:])
    ot = ot * lax.rsqrt(jnp.mean(ot * ot, axis=0, keepdims=True) + EPS)
    o_ref[...] = ot.T * lnw_ref[...] * out_scale


def _prompt_attention(lam, qt, kab, vt, diff_ln_w, out_scale, b, l, t):
    tile = qt.shape[2]
    assert l % t == 0 and t % tile == 0
    nq = l // t
    r = t // tile
    nt = l // tile
    k3 = kab.reshape(b * nt, tile, SEG)
    out = pl.pallas_call(
        functools.partial(_pattn_body, t=t, r=r, out_scale=out_scale),
        grid=(b, H_A, nq),
        in_specs=[
            pl.BlockSpec((1, 1), lambda bi, h, i: (0, 0)),
            pl.BlockSpec((r, 128, tile), lambda bi, h, i: (bi * nq + i, h, 0)),
            pl.BlockSpec((nt, tile, 128), lambda bi, h, i: (bi, 0, h)),
            pl.BlockSpec((nt, 128, tile), lambda bi, h, i: (bi, h, 0)),
            pl.BlockSpec((1, DV_A), lambda bi, h, i: (0, 0)),
        ],
        out_specs=pl.BlockSpec((None, t, 128), lambda bi, h, i: (bi, i, h)),
        out_shape=jax.ShapeDtypeStruct((b, l, SEG), F32),
        scratch_shapes=[pltpu.VMEM((1, 2 * t), F32), pltpu.VMEM((1, 2 * t), F32), pltpu.VMEM((128, 2 * t), F32)],
        compiler_params=_cparams("parallel", "parallel", "arbitrary"),
        name="prompt_attn",
    )(lam, qt, k3, vt, diff_ln_w.reshape(1, DV_A))
    return out.reshape(b * l, SEG)


def _sattn_body(pt_ref, lam_ref, q_ref, *refs, n_grp, ls, out_scale):
    k_refs = refs[:n_grp]
    v_refs = refs[n_grp:2 * n_grp]
    kn_ref, vn_ref, lnw_ref, o_ref, q2_ref, m_ref, l_ref, acc_ref = refs[2 * n_grp:]
    j = pl.program_id(1)
    rows = 2 * ls

    @pl.when(j == 0)
    def _():
        q = q_ref[...]
        for h in range(H_A):
            q2_ref[h * rows:(h + 1) * rows, :] = _split_maps(q[:, h * 128:(h + 1) * 128])
        m_ref[...] = jnp.full(m_ref.shape, NEG_INF, F32)
        l_ref[...] = jnp.zeros(l_ref.shape, F32)
        acc_ref[...] = jnp.zeros(acc_ref.shape, F32)

    def head_update(h, kt, v, mask):
        hs = slice(h * rows, (h + 1) * rows)
        s = _dot(q2_ref[hs, :], kt)
        if mask is not None:
            s = jnp.where(mask, s, NEG_INF)
        _online_update(s, v, m_ref.at[hs, :], l_ref.at[hs, :], acc_ref.at[hs, :])

    for h in range(H_A):
        hl = slice(h * 128, (h + 1) * 128)
        head_update(h, jnp.concatenate([r[hl, :].astype(BF16) for r in k_refs], axis=1),
                    jnp.concatenate([r[pl.ds(h, PAGE_SIZE, stride=H_A), :].astype(BF16) for r in v_refs], axis=0),
                    None)

    @pl.when(j == pl.num_programs(1) - 1)
    def _():
        knt = kn_ref[...]
        vn = vn_ref[...]
        r = lax.broadcasted_iota(jnp.int32, (rows, PAGE_SIZE), 0)
        r = jnp.where(r >= ls, r - ls, r)
        c = lax.broadcasted_iota(jnp.int32, (rows, PAGE_SIZE), 1)
        mask = c <= r
        for h in range(H_A):
            hl = slice(h * 128, (h + 1) * 128)
            hs = slice(h * rows, (h + 1) * rows)
            head_update(h, knt[hl, :], vn[:, hl], mask)
            o_ref[:, hl] = _diff_finish(acc_ref[hs, :], l_ref[hs, :], ls, lam_ref[...], lnw_ref[...], out_scale)


def _sample_attention(lam, qab, kab, vab, cache_k, cache_v, page_table, diff_ln_w, out_scale, db, ls, n_grp):
    n_pages = page_table.shape[1]
    assert n_pages % n_grp == 0
    n_pool = cache_k.shape[0]
    ck = jnp.transpose(cache_k, (0, 2, 3, 4, 1)).reshape(n_pool, SEG, PAGE_SIZE)
    cv = cache_v.reshape(n_pool, PAGE_SIZE * H_A, DV_A)
    q3 = qab.reshape(db, ls, SEG)
    kn = jnp.pad(jnp.transpose(kab.reshape(db, ls, SEG), (0, 2, 1)), ((0, 0), (0, 0), (0, PAGE_SIZE - ls)))
    vn = jnp.pad(vab.reshape(db, ls, SEG), ((0, 0), (0, PAGE_SIZE - ls), (0, 0)))

    def page_spec(g, shape):
        return pl.BlockSpec((None,) + shape, lambda bi, j, pt: (pt[bi * n_pages + j * n_grp + g], 0, 0))

    per_b = pl.BlockSpec((None, ls, SEG), lambda bi, j, pt: (bi, 0, 0))
    rows = 2 * ls * H_A
    grid_spec = pltpu.PrefetchScalarGridSpec(
        num_scalar_prefetch=1,
        grid=(db, n_pages // n_grp),
        in_specs=[pl.BlockSpec((1, 1), lambda bi, j, pt: (0, 0)), per_b]
        + [page_spec(g, (SEG, PAGE_SIZE)) for g in range(n_grp)]
        + [page_spec(g, (PAGE_SIZE * H_A, DV_A)) for g in range(n_grp)]
        + [pl.BlockSpec((None, SEG, PAGE_SIZE), lambda bi, j, pt: (bi, 0, 0)),
           pl.BlockSpec((None, PAGE_SIZE, SEG), lambda bi, j, pt: (bi, 0, 0)),
           pl.BlockSpec((1, DV_A), lambda bi, j, pt: (0, 0))],
        out_specs=per_b,
        scratch_shapes=[pltpu.VMEM((rows, 128), BF16), pltpu.VMEM((rows, 1), F32),
                        pltpu.VMEM((rows, 1), F32), pltpu.VMEM((rows, 128), F32)],
    )
    out = pl.pallas_call(
        functools.partial(_sattn_body, n_grp=n_grp, ls=ls, out_scale=out_scale),
        grid_spec=grid_spec,
        out_shape=jax.ShapeDtypeStruct((db, ls, SEG), F32),
        compiler_params=_cparams("parallel", "arbitrary"),
        name="sample_attn",
    )(page_table.reshape(-1), lam, q3, *([ck] * n_grp), *([cv] * n_grp), kn, vn, diff_ln_w.reshape(1, DV_A))
    return out.reshape(db * ls, SEG)


def _hgrn_body(q_ref, k_ref, v_ref, lf_ref, g_ref, s0_ref, nw_ref, o_ref, sout_ref, st_ref, b_ref,
               *, chunk, sub, n_chunks):
    t = pl.program_id(1)

    @pl.when(t == 0)
    def _():
        for h in range(H_B):
            st_ref[h] = s0_ref[h].T

    r = lax.broadcasted_iota(jnp.int32, (chunk, chunk), 0)
    c = lax.broadcasted_iota(jnp.int32, (chunk, chunk), 1)
    tri = (c <= r).astype(F32)
    nw = nw_ref[...]

    n_sub = chunk // sub
    heads = range(H_B)
    hl = [slice(h * DK_B, (h + 1) * DK_B) for h in heads]

    def one_chunk(ci, carry):
        rows = pl.ds(pl.multiple_of(ci * chunk, chunk), chunk)
        b_ref[...] = jnp.dot(tri, lf_ref[rows, :], precision=lax.Precision.HIGHEST, preferred_element_type=F32)
        q = [q_ref[rows, hl[h]] for h in heads]
        k = [k_ref[rows, hl[h]] for h in heads]
        v = [v_ref[rows, hl[h]] for h in heads]
        b = [b_ref[:, hl[h]] for h in heads]
        st = [st_ref[h] for h in heads]
        att = {}
        for h in heads:
            for j in range(n_sub):
                ref = b_ref[j * sub - 1:j * sub, hl[h]] if j > 0 else jnp.zeros((1, DK_B), F32)
                ncol = (j + 1) * sub
                qj = (q[h][j * sub:ncol] * jnp.exp(b[h][j * sub:ncol] - ref)).astype(BF16)
                kj = (k[h][:ncol] * jnp.exp(ref - b[h][:ncol])).astype(BF16)
                att[h, j] = _dot_nt(qj, kj)
        inter = [_dot_nt((q[h] * jnp.exp(b[h])).astype(BF16), st[h].astype(BF16)) for h in heads]
        b_last = [b[h][chunk - 1:chunk, :] for h in heads]
        upd = [_dot(v[h].T, (k[h] * jnp.exp(b_last[h] - b[h])).astype(BF16)) for h in heads]
        for h in heads:
            o_parts = []
            for j in range(n_sub):
                ncol = (j + 1) * sub
                causal = (lax.broadcasted_iota(jnp.int32, (sub, ncol), 1)
                          <= lax.broadcasted_iota(jnp.int32, (sub, ncol), 0) + j * sub)
                a = jnp.where(causal, att[h, j], 0.0)
                o_parts.append(_dot(a.astype(BF16), v[h][:ncol]))
            o = jnp.concatenate(o_parts, axis=0) + inter[h]
            st_ref[h] = st[h] * jnp.exp(b_last[h]) + upd[h]
            gate = g_ref[rows, hl[h]]
            o_ref[rows, hl[h]] = _rms(o, nw) * (gate * jax.nn.sigmoid(gate))
        return carry

    lax.fori_loop(0, n_chunks, one_chunk, 0)

    @pl.when(t == pl.num_programs(1) - 1)
    def _():
        for h in range(H_B):
            sout_ref[h] = st_ref[h].T


def _hgrn(qb, kb, ib, lf, gb, s0, norm_w, b, l, chunk, n_chunks):
    rows = chunk * n_chunks
    assert l % rows == 0
    sub = min(HGRN_SUB, chunk)
    a3 = lambda t: t.reshape(b, l, SEG)
    tok = pl.BlockSpec((None, rows, SEG), lambda bi, t: (bi, t, 0))
    st = pl.BlockSpec((None, H_B, DK_B, DV_B), lambda bi, t: (bi, 0, 0, 0))
    out, s_fin = pl.pallas_call(
        functools.partial(_hgrn_body, chunk=chunk, sub=sub, n_chunks=n_chunks),
        grid=(b, l // rows),
        in_specs=[tok, tok, tok, tok, tok, st, pl.BlockSpec((1, DV_B), lambda bi, t: (0, 0))],
        out_specs=[tok, st],
        out_shape=[jax.ShapeDtypeStruct((b, l, SEG), F32), jax.ShapeDtypeStruct((b, H_B, DK_B, DV_B), F32)],
        scratch_shapes=[pltpu.VMEM((H_B, DV_B, DK_B), F32), pltpu.VMEM((chunk, SEG), F32)],
        compiler_params=_cparams("parallel", "arbitrary"),
        name="hgrn2",
    )(a3(qb), a3(kb), a3(ib), a3(lf), a3(gb), s0, norm_w.reshape(1, DV_B))
    return out.reshape(b * l, SEG), s_fin


def _mix_body(x_ref, oa_ref, ob_ref, w_ref, o_ref):
    o_ref[...] = (x_ref[...] + _dot(oa_ref[...].astype(BF16), w_ref[:SEG, :])
                  + _dot(ob_ref[...].astype(BF16), w_ref[SEG:, :]))


def _mix(x2d, oa, ob, w_out_bf, tm):
    n = x2d.shape[0]
    row = lambda i: (i, 0)
    return pl.pallas_call(
        _mix_body,
        grid=(n // tm,),
        in_specs=[pl.BlockSpec((tm, D_MODEL), row), pl.BlockSpec((tm, SEG), row), pl.BlockSpec((tm, SEG), row),
                  pl.BlockSpec((D_MODEL, D_MODEL), lambda i: (0, 0))],
        out_specs=pl.BlockSpec((tm, D_MODEL), row),
        out_shape=jax.ShapeDtypeStruct((n, D_MODEL), F32),
        compiler_params=_cparams("parallel"),
        name="mix_out",
    )(x2d, oa, ob, w_out_bf)


def _memkv_body(m_ref, nw_ref, wk_ref, wv_ref, k_ref, v_ref):
    hb = _rms(m_ref[...], nw_ref[...]).astype(BF16)
    k_ref[...] = _dot(hb, wk_ref[...])
    v_ref[...] = _dot(hb, wv_ref[...])


def _mem_kv(mem2d, norm_w, wk_bf, wv_bf, tm):
    n = mem2d.shape[0]
    row = lambda i: (i, 0)
    const = lambda i: (0, 0)
    o = jax.ShapeDtypeStruct((n, D_MODEL), F32)
    return pl.pallas_call(
        _memkv_body,
        grid=(n // tm,),
        in_specs=[pl.BlockSpec((tm, D_MODEL), row), pl.BlockSpec((1, D_MODEL), const),
                  pl.BlockSpec((D_MODEL, D_MODEL), const), pl.BlockSpec((D_MODEL, D_MODEL), const)],
        out_specs=[pl.BlockSpec((tm, D_MODEL), row)] * 2,
        out_shape=[o, o],
        compiler_params=_cparams("parallel"),
        name="mem_kv",
    )(mem2d, norm_w.reshape(1, D_MODEL), wk_bf, wv_bf)


def _memattn_body(x_ref, nw_ref, wq_ref, mk_ref, mv_ref, wo_ref, o_ref, *, head_rows):
    x = x_ref[...]
    q = _dot(_rms(x, nw_ref[...]).astype(BF16), wq_ref[...])
    q = (q * (DH_M ** -0.5)).astype(BF16)
    heads = []

    def head_of(ref, h):
        halves = [ref[pl.ds(half * H_M + h, MEM_LEN, stride=2 * H_M), :] for half in range(2)]
        return jnp.concatenate(halves, axis=1).astype(BF16)

    for h in range(H_M):
        hl = slice(h * DH_M, (h + 1) * DH_M)
        if head_rows:
            mk, mv = head_of(mk_ref, h), head_of(mv_ref, h)
        else:
            mk, mv = mk_ref[:, hl].astype(BF16), mv_ref[:, hl].astype(BF16)
        s = _dot_nt(q[:, hl], mk)
        p = jnp.exp(s - jnp.max(s, axis=-1, keepdims=True))
        o = _dot(p.astype(BF16), mv)
        heads.append(o / jnp.sum(p, axis=-1, keepdims=True))
    o_ref[...] = x + _dot(jnp.concatenate(heads, axis=1).astype(BF16), wo_ref[...])


def _mem_attend(x2d, norm_w, wq_bf, mk, mv, wo_bf, tm, tiles_per_batch):
    n = x2d.shape[0]
    row = lambda i: (i, 0)
    const = lambda i: (0, 0)
    head_rows = mk.shape[1] == MEM_LEN * 2 * H_M
    mem = pl.BlockSpec((None,) + mk.shape[1:], lambda i: (i // tiles_per_batch, 0, 0))
    return pl.pallas_call(
        functools.partial(_memattn_body, head_rows=head_rows),
        grid=(n // tm,),
        in_specs=[pl.BlockSpec((tm, D_MODEL), row), pl.BlockSpec((1, D_MODEL), const),
                  pl.BlockSpec((D_MODEL, D_MODEL), const), mem, mem, pl.BlockSpec((D_MODEL, D_MODEL), const)],
        out_specs=pl.BlockSpec((tm, D_MODEL), row),
        out_shape=jax.ShapeDtypeStruct((n, D_MODEL), F32),
        compiler_params=_cparams("parallel"),
        name="mem_attn",
    )(x2d, norm_w.reshape(1, D_MODEL), wq_bf, mk, mv, wo_bf)


_REMOVED_EXP = 100
_REMOVED_BITS = ((_REMOVED_EXP + 127) << 23) - (1 << 31)


def _extract_topk(work_ref, rank_ref, vals_ref, n_rows, k, exact):
    shape = work_ref.shape
    row = lax.broadcasted_iota(jnp.int32, shape, 0)

    def body(a, carry):
        w = work_ref[...]
        m = jnp.max(w, axis=0, keepdims=True)
        if exact:
            hit = row == jnp.min(jnp.where(w == m, row, n_rows), axis=0, keepdims=True)
        else:
            hit = w == m
        marker = pltpu.bitcast(jnp.full(shape, _REMOVED_BITS, jnp.int32) + (a << 23), F32)
        work_ref[...] = jnp.where(hit, marker, w)
        vals_ref[pl.ds(a, 1), :] = m
        return carry

    lax.fori_loop(0, k, body, 0)
    bits = pltpu.bitcast(work_ref[...], jnp.int32)
    order = ((bits >> 23) & 0xFF) - (_REMOVED_EXP + 127)
    removed = work_ref[...] <= -(2.0 ** _REMOVED_EXP)
    rank = jnp.where(removed, order, k).astype(F32)
    rank_ref[...] = rank
    n_removed = jnp.sum((rank < float(k)).astype(F32), axis=0, keepdims=True)
    return (n_removed > float(k)).astype(F32)


def _peer_select_body(x_ref, nw_ref, wq_ref, keys_ref, xn_ref, r1_ref, n2_ref, e1_ref, e2_ref, qt_ref,
                      work_ref, rank1_ref, rank2_ref, vals1_ref, vals2_ref, cand_ref, crank_ref, cvals_ref):
    hbt = _rms(x_ref[...], nw_ref[...]).T.astype(BF16)
    xn_ref[...] = hbt
    qt_ref[...] = _dot(wq_ref[...], hbt).astype(BF16)
    k = PEER_TOPK

    def select_head(h, exact):
        tie = jnp.zeros((1, qt_ref.shape[1]), F32)
        e_half = []
        for p, (rank_ref, vals_ref) in enumerate(((rank1_ref, vals1_ref), (rank2_ref, vals2_ref))):
            hp = 2 * h + p
            s = _dot(keys_ref[hp], qt_ref[hp * PEER_HALF:(hp + 1) * PEER_HALF, :])
            e_half.append(jnp.exp(s - jnp.max(s, axis=0, keepdims=True)))
            work_ref[...] = s
            tie = jnp.maximum(tie, _extract_topk(work_ref, rank_ref, vals_ref, N_KEYS, k, exact))
        v1 = vals1_ref[...]
        v2 = vals2_ref[...]
        cand_ref[...] = jnp.full(cand_ref.shape, NEG_INF, F32)
        for a, (off, nb_a) in enumerate(_CAND_ROWS):
            cand_ref[off:off + nb_a, :] = v1[a:a + 1, :] + v2[:nb_a, :]
        tie = jnp.maximum(tie, _extract_topk(cand_ref, crank_ref, cvals_ref, _N_CAND_PAD, k, exact))
        cv = cvals_ref[...]
        z = jnp.sum(jnp.exp(cv - cv[0:1, :]), axis=0, keepdims=True)
        sel = (crank_ref[...] < float(k)).astype(F32)
        tt = sel.shape[1]
        nb = None
        for off, nb_a in _CAND_ROWS:
            rows = sel[off:off + nb_a, :]
            if nb_a < k:
                rows = jnp.concatenate([rows, jnp.zeros((k - nb_a, tt), F32)], axis=0)
            nb = rows if nb is None else nb + rows
        rank2 = rank2_ref[...]
        n2 = jnp.zeros(rank2.shape, F32)
        for b in range(k):
            n2 = jnp.where(rank2 == float(b), nb[b:b + 1, :], n2)
        r1_ref[h] = rank1_ref[...]
        n2_ref[h] = n2.astype(BF16)
        e1_ref[h] = e_half[0]
        e2_ref[h] = (e_half[1] / z).astype(BF16)
        return tie

    for h in range(PEER_HEADS):
        tie = select_head(h, False)

        @pl.when(jnp.max(tie) > 0.0)
        def _():
            select_head(h, True)


def _cand_rows(k):
    rows, off = [], 0
    for a in range(k):
        nb_a = k // (a + 1)
        rows.append((off, nb_a))
        off += nb_a
    return tuple(rows), off


_CAND_ROWS, _N_CAND = _cand_rows(PEER_TOPK)
_N_CAND_PAD = -(-_N_CAND // 8) * 8


def _peer_select(x2d, norm_w, wq_bf, keys_bf, tt):
    n = x2d.shape[0]
    k = PEER_TOPK
    const2 = lambda i: (0, 0)
    tokmajor = pl.BlockSpec((PEER_HEADS, N_KEYS, tt), lambda i: (0, 0, i))
    kt = jax.ShapeDtypeStruct((PEER_HEADS, N_KEYS, n), F32)
    kt16 = jax.ShapeDtypeStruct((PEER_HEADS, N_KEYS, n), BF16)
    return pl.pallas_call(
        _peer_select_body,
        grid=(n // tt,),
        in_specs=[pl.BlockSpec((tt, D_MODEL), lambda i: (i, 0)), pl.BlockSpec((1, D_MODEL), const2),
                  pl.BlockSpec(wq_bf.shape, const2), pl.BlockSpec(keys_bf.shape, lambda i: (0, 0, 0))],
        out_specs=[pl.BlockSpec((D_MODEL, tt), lambda i: (0, i)), tokmajor, tokmajor, tokmajor, tokmajor],
        out_shape=[jax.ShapeDtypeStruct((D_MODEL, n), BF16), kt, kt16, kt, kt16],
        scratch_shapes=[pltpu.VMEM((2 * PEER_HEADS * PEER_HALF, tt), BF16),
                        pltpu.VMEM((N_KEYS, tt), F32), pltpu.VMEM((N_KEYS, tt), F32), pltpu.VMEM((N_KEYS, tt), F32),
                        pltpu.VMEM((k, tt), F32), pltpu.VMEM((k, tt), F32),
                        pltpu.VMEM((_N_CAND_PAD, tt), F32), pltpu.VMEM((_N_CAND_PAD, tt), F32),
                        pltpu.VMEM((k, tt), F32)],
        compiler_params=_cparams("parallel"),
        name="peer_select",
    )(x2d, norm_w.reshape(1, D_MODEL), wq_bf, keys_bf)


def _peer_dense_body(x_ref, xnt_ref, r1_ref, n2_ref, e1_ref, e2_ref, u_ref, vt_ref, fw_ref, o_ref,
                     yt_ref, ht_ref, *, blocks, final_norm):
    c = pl.program_id(1)

    @pl.when(c == 0)
    def _():
        yt_ref[...] = jnp.zeros(yt_ref.shape, F32)

    tt = xnt_ref.shape[1]
    pre = _dot(u_ref[...], xnt_ref[...])
    act = (0.5 * pre * (1.0 + lax.erf(pre * (2.0 ** -0.5)))).astype(BF16)
    reps = N_KEYS // _ROW_BCAST

    def bcast_row(ref, h, g):
        row = jnp.broadcast_to(ref[h, g:g + 1, :], (_ROW_BCAST, tt)).astype(BF16)
        return jnp.concatenate([row] * reps, axis=0)

    for g in range(blocks):
        w = None
        for h in range(PEER_HEADS):
            r1 = bcast_row(r1_ref, h, g)
            gate = bcast_row(e1_ref, h, g) * e2_ref[h]
            term = jnp.where(r1 < n2_ref[h], gate, jnp.zeros_like(gate))
            w = term if w is None else w + term
        ht_ref[g * N_KEYS:(g + 1) * N_KEYS, :] = w * act[g * N_KEYS:(g + 1) * N_KEYS, :]
    yt_ref[...] += _dot(vt_ref[...], ht_ref[...])

    @pl.when(c == pl.num_programs(1) - 1)
    def _():
        y = x_ref[...] + yt_ref[...].T
        o_ref[...] = _rms(y, fw_ref[...]) if final_norm else y


_ROW_BCAST = 16


def _peer_dense(x2d, xnt, r1, n2, e1, e2, u_bf, vt_bf, final_w, tt, blocks, final_norm):
    n = x2d.shape[0]
    n_exp = u_bf.shape[0]
    ec = blocks * N_KEYS
    assert blocks % 8 == 0 and n_exp == N_KEYS * N_KEYS
    tok = pl.BlockSpec((tt, D_MODEL), lambda i, c: (i, 0))
    rows = pl.BlockSpec((PEER_HEADS, blocks, tt), lambda i, c: (0, c, i))
    tiles = pl.BlockSpec((PEER_HEADS, N_KEYS, tt), lambda i, c: (0, 0, i))
    return pl.pallas_call(
        functools.partial(_peer_dense_body, blocks=blocks, final_norm=final_norm),
        grid=(n // tt, n_exp // ec),
        in_specs=[tok, pl.BlockSpec((D_MODEL, tt), lambda i, c: (0, i)), rows, tiles, rows, tiles,
                  pl.BlockSpec((ec, D_MODEL), lambda i, c: (c, 0)),
                  pl.BlockSpec((D_MODEL, ec), lambda i, c: (0, c)),
                  pl.BlockSpec((1, D_MODEL), lambda i, c: (0, 0))],
        out_specs=tok,
        out_shape=jax.ShapeDtypeStruct((n, D_MODEL), F32),
        scratch_shapes=[pltpu.VMEM((D_MODEL, tt), F32), pltpu.VMEM((ec, tt), BF16)],
        compiler_params=_cparams("parallel", "arbitrary"),
        name="peer_dense",
    )(x2d, xnt, r1, n2, e1, e2, u_bf, vt_bf, final_w.reshape(1, D_MODEL))


def _peer(x2d, norm_w, wqt_bf, keys_bf, u_bf, vt_bf, final_w, final_norm, t_sel, t_dense, blocks):
    xnt, r1, n2, e1, e2 = _peer_select(x2d, norm_w, wqt_bf, keys_bf, t_sel)
    return _peer_dense(x2d, xnt, r1, n2, e1, e2, u_bf, vt_bf, final_w, t_dense, blocks, final_norm)


def _tile(n, want):
    return min(n, want)


def kernel(x_prompt, x_sample, cache_attn_k, cache_attn_v, cache_mem_k, cache_mem_v, state_hgrn, page_table, mem_prompt, norm_mix_w, w_in, lambda_q1, lambda_k1, lambda_q2, lambda_k2, diff_ln_w, hgrn_lower_bounds, hgrn_norm_w, w_out, norm_mem_q_w, norm_mem_kv_w, w_mq, w_mk, w_mv, w_mo, norm_ffn_w, peer_wq, peer_keys, peer_u, peer_v, final_norm_w):
    b, l = x_prompt.shape[:2]
    db, ls = x_sample.shape[:2]
    depth = w_in.shape[0]
    past = page_table.shape[1] * PAGE_SIZE
    pos_p = jnp.arange(l)
    tm_s = _tile(db * ls, 256)
    pos_s = past + (jnp.arange(tm_s) % ls)
    xp = x_prompt.reshape(b * l, D_MODEL)
    xs = x_sample.reshape(db * ls, D_MODEL)
    chunk_p = HGRN_CHUNK if l % HGRN_CHUNK == 0 else l
    chunk_s = HGRN_CHUNK if ls % HGRN_CHUNK == 0 else ls
    tq = _tile(l, 512)
    outs = [[] for _ in range(8)]
    for layer in range(depth):
        lam_init = 0.8 - 0.6 * math.exp(-0.3 * layer)
        lam = (jnp.exp(jnp.sum(lambda_q1[layer] * lambda_k1[layer]))
               - jnp.exp(jnp.sum(lambda_q2[layer] * lambda_k2[layer])) + lam_init).reshape(1, 1).astype(F32)
        last = layer == depth - 1
        w_in_bf = w_in[layer].astype(BF16)
        w_out_bf = w_out[layer].astype(BF16)
        wmq, wmk, wmv, wmo = (w[layer].astype(BF16) for w in (w_mq, w_mk, w_mv, w_mo))
        wqt_bf = peer_wq[layer].astype(BF16).T
        keys_bf = peer_keys[layer].astype(BF16).reshape(PEER_HEADS * 2, N_KEYS, PEER_HALF)
        u_bf = peer_u[layer].astype(BF16)
        vt_bf = peer_v[layer].astype(BF16).T
        peer = functools.partial(_peer, norm_w=norm_ffn_w[layer], wqt_bf=wqt_bf, keys_bf=keys_bf, u_bf=u_bf,
                                 vt_bf=vt_bf, final_w=final_norm_w, final_norm=last)

        kat, va, qat, kab, vat, qb, kb, ib, lf, gb = _inproj(
            xp, norm_mix_w[layer], w_in_bf, hgrn_lower_bounds, pos_p, layer, tq, True)
        oa = _prompt_attention(lam, qat, kab, vat, diff_ln_w[layer], 1.0 - lam_init, b, l, _tile(l, 1024))
        n_chunks = max(1, min(l, 512) // chunk_p)
        ob, sp = _hgrn(qb, kb, ib, lf, gb, jnp.zeros((b, H_B, DK_B, DV_B), F32), hgrn_norm_w[layer],
                       b, l, chunk_p, n_chunks)
        xp = _mix(xp, oa, ob, w_out_bf, _tile(l, 512))
        mk, mv = _mem_kv(mem_prompt.reshape(b * MEM_LEN, D_MODEL), norm_mem_kv_w[layer], wmk, wmv, MEM_LEN)
        tm = _tile(l, 512)
        xp = _mem_attend(xp, norm_mem_q_w[layer], wmq, mk.reshape(b, MEM_LEN, D_MODEL),
                         mv.reshape(b, MEM_LEN, D_MODEL), wmo, tm, l // tm)
        xp = peer(xp, t_sel=_tile(b * l, 512), t_dense=_tile(b * l, 512), blocks=8)
        outs[0].append(jnp.transpose(kat.reshape(b, H_A, 2, DK_A, l), (0, 4, 1, 2, 3)))
        outs[1].append(va.reshape(b, l, H_A, DV_A))
        outs[2].append(sp)
        outs[3].append(mk.reshape(b, MEM_LEN, H_M, DH_M))
        outs[4].append(mv.reshape(b, MEM_LEN, H_M, DH_M))

        ka, va, qab, kab, vab, qb, kb, ib, lf, gb = _inproj(
            xs, norm_mix_w[layer], w_in_bf, hgrn_lower_bounds, pos_s, layer, tm_s, False)
        oa = _sample_attention(lam, qab, kab, vab, cache_attn_k[layer], cache_attn_v[layer], page_table,
                               diff_ln_w[layer], 1.0 - lam_init, db, ls, min(32, page_table.shape[1]))
        ob, ss = _hgrn(qb, kb, ib, lf, gb, state_hgrn[layer], hgrn_norm_w[layer], db, ls, chunk_s, ls // chunk_s)
        xs = _mix(xs, oa, ob, w_out_bf, tm_s)
        mem_view = lambda c: jnp.transpose(c.reshape(db, MEM_LEN, H_M, 2, DH_M // 2),
                                           (0, 1, 3, 2, 4)).reshape(db, MEM_LEN * 2 * H_M, DH_M // 2)
        xs = _mem_attend(xs, norm_mem_q_w[layer], wmq, mem_view(cache_mem_k[layer]),
                         mem_view(cache_mem_v[layer]), wmo, ls, 1)
        xs = peer(xs, t_sel=_tile(db * ls, 512), t_dense=_tile(db * ls, 512), blocks=8)
        outs[5].append(ka.reshape(db, ls, H_A, 2, DK_A))
        outs[6].append(va.reshape(db, ls, H_A, DV_A))
        outs[7].append(ss)
    y_prompt = xp.reshape(b, l, D_MODEL)
    y_sample = xs.reshape(db, ls, D_MODEL)
    return (y_prompt, y_sample) + tuple(jnp.stack(o) for o in outs)
```

```python
import functools
import math

import jax
import jax.numpy as jnp
from jax import lax
from jax.experimental import pallas as pl
from jax.experimental.pallas import tpu as pltpu

F32 = jnp.float32
BF16 = jnp.bfloat16
EPS = 1e-6
NEG_INF = float("-inf")

D_MODEL = 1024
PAGE_SIZE = 128
H_A = 4
DV_A = 128
DK_A = 64
ROT_DIM = 16
ROPE_THETA = 500000.0
H_B = 4
DK_B = 128
DV_B = 128
HGRN_CHUNK = 64
HGRN_SUB = 16
MEM_LEN = 256
H_M = 4
DH_M = 256
N_KEYS = 128
PEER_HEADS = 8
PEER_TOPK = 16
PEER_HALF = 128
SEG = 512
N_SEG = 7

VMEM_LIMIT = 56 * 1024 * 1024


def _cparams(*sem, flags=None):
    return pltpu.CompilerParams(dimension_semantics=sem, vmem_limit_bytes=VMEM_LIMIT, flags=flags)


def _rms(x, w):
    return x * lax.rsqrt(jnp.mean(x * x, axis=-1, keepdims=True) + EPS) * w


def _dot_nt(a, b):
    return lax.dot_general(a, b, (((1,), (1,)), ((), ())), preferred_element_type=F32)


def _dot(a, b):
    return jnp.dot(a, b, preferred_element_type=F32)


def _inproj_body(x_ref, nw_ref, w_ref, lbp_ref, rc_ref, rs1_ref, rs2_ref,
                 ka_ref, va_ref, qab_ref, kab_ref, vab_ref,
                 qb_ref, kb_ref, ib_ref, lf_ref, gb_ref, *, layer, transposed):
    hb = _rms(x_ref[...], nw_ref[...]).astype(BF16)

    def seg(i):
        return _dot(hb, w_ref[:, i * SEG:(i + 1) * SEG])

    rc, rs1, rs2 = rc_ref[...], rs1_ref[...], rs2_ref[...]

    def rope_block(blk):
        return blk * rc + pltpu.roll(blk, 128 - ROT_DIM // 2, 1) * rs1 + pltpu.roll(blk, ROT_DIM // 2, 1) * rs2

    qa = seg(0)
    ka = seg(1)
    for g in range(SEG // 128):
        sl = slice(g * 128, (g + 1) * 128)
        qr = rope_block(qa[:, sl]) * (DK_A ** -0.5)
        if transposed:
            qab_ref[sl, :] = qr.T.astype(BF16)
        else:
            qab_ref[:, sl] = qr.astype(BF16)
        kr = rope_block(ka[:, sl])
        if transposed:
            ka_ref[sl, :] = kr.T
        else:
            ka_ref[:, sl] = kr
        kab_ref[:, sl] = kr.astype(BF16)
    va = seg(2)
    va_ref[...] = va
    if transposed:
        for g in range(SEG // 128):
            sl = slice(g * 128, (g + 1) * 128)
            vab_ref[sl, :] = va[:, sl].T.astype(BF16)
    else:
        vab_ref[...] = va.astype(BF16)
    qb_ref[...] = seg(3)
    lbp = lbp_ref[...]
    e = jnp.exp(lbp - jnp.max(lbp, axis=0, keepdims=True))
    lb = jnp.sum(e[:layer + 1], axis=0, keepdims=True) / jnp.sum(e, axis=0, keepdims=True)
    fg = lb + (1.0 - lb) * jax.nn.sigmoid(seg(4))
    lf_ref[...] = jnp.log(fg)
    kb_ref[...] = 1.0 - fg
    ib_ref[...] = seg(5).astype(BF16)
    gb_ref[...] = seg(6)


def _rope_tables(pos):
    half = ROT_DIM // 2
    inv = ROPE_THETA ** (-jnp.arange(half, dtype=F32) * 2.0 / ROT_DIM)
    ang = pos.astype(F32)[:, None] * inv[None, :]
    cos, sin = jnp.cos(ang), jnp.sin(ang)
    p = pos.shape[0]
    ones = jnp.ones((p, DK_A - ROT_DIM), F32)
    zeros = jnp.zeros((p, DK_A - ROT_DIM), F32)
    zh = jnp.zeros((p, half), F32)
    rc = jnp.concatenate([cos, cos, ones], axis=1)
    rs1 = jnp.concatenate([-sin, zh, zeros], axis=1)
    rs2 = jnp.concatenate([zh, sin, zeros], axis=1)
    return tuple(jnp.concatenate([t, t], axis=1) for t in (rc, rs1, rs2))


def _inproj(x2d, norm_w, w_in_bf, lower_bounds, pos, layer, tm, transposed):
    n = x2d.shape[0]
    p = pos.shape[0]
    assert n % tm == 0 and p % tm == 0
    npb = p // tm
    rc, rs1, rs2 = _rope_tables(pos)
    row = lambda i: (i, 0)
    const = lambda i: (0, 0)
    tab = lambda i: (i % npb, 0)
    f32o = jax.ShapeDtypeStruct((n, SEG), F32)
    bf16o = jax.ShapeDtypeStruct((n, SEG), BF16)
    out_spec = pl.BlockSpec((tm, SEG), row)
    out_specs = [out_spec] * 10
    out_shape = [f32o, f32o, bf16o, bf16o, bf16o, f32o, f32o, bf16o, f32o, f32o]
    if transposed:
        t_spec = pl.BlockSpec((None, SEG, tm), lambda i: (i, 0, 0))
        t_shape = jax.ShapeDtypeStruct((n // tm, SEG, tm), BF16)
        out_specs[2] = out_specs[4] = t_spec
        out_shape[2] = out_shape[4] = t_shape
        out_specs[0] = pl.BlockSpec((None, SEG, tm), lambda i: (i // npb, 0, i % npb))
        out_shape[0] = jax.ShapeDtypeStruct((n // p, SEG, p), F32)
    return pl.pallas_call(
        functools.partial(_inproj_body, layer=layer, transposed=transposed),
        grid=(n // tm,),
        in_specs=[
            pl.BlockSpec((tm, D_MODEL), row),
            pl.BlockSpec((1, D_MODEL), const),
            pl.BlockSpec((D_MODEL, N_SEG * SEG), const),
            pl.BlockSpec(lower_bounds.shape, const),
            pl.BlockSpec((tm, 128), tab),
            pl.BlockSpec((tm, 128), tab),
            pl.BlockSpec((tm, 128), tab),
        ],
        out_specs=out_specs,
        out_shape=out_shape,
        compiler_params=_cparams("parallel"),
        name="inproj",
    )(x2d, norm_w.reshape(1, D_MODEL), w_in_bf, lower_bounds, rc, rs1, rs2)


def _split_maps(q):
    lane = lax.broadcasted_iota(jnp.int32, q.shape, 1)
    zero = jnp.zeros_like(q)
    return jnp.concatenate([jnp.where(lane < DK_A, q, zero), jnp.where(lane >= DK_A, q, zero)], axis=0)


def _online_update(s, v, m_ref, l_ref, acc_ref):
    m_prev = m_ref[...]
    m_new = jnp.maximum(m_prev, jnp.max(s, axis=-1, keepdims=True))
    alpha = jnp.exp(m_prev - m_new)
    p = jnp.exp(s - m_new)
    l_ref[...] = alpha * l_ref[...] + jnp.sum(p, axis=-1, keepdims=True)
    acc_ref[...] = alpha * acc_ref[...] + _dot(p.astype(BF16), v)
    m_ref[...] = m_new


def _diff_finish(acc, l, t, lam, lnw, out_scale):
    o = acc[:t] / l[:t] - lam * (acc[t:] / l[t:])
    return _rms(o, lnw) * out_scale


def _pattn_body(lam_ref, qt_ref, k_ref, vt_ref, lnw_ref, o_ref, m_ref, l_ref, acc_ref, *, t, r, out_scale):
    i = pl.program_id(2)
    qt = jnp.concatenate([qt_ref[c] for c in range(r)], axis=1)
    feat = lax.broadcasted_iota(jnp.int32, qt.shape, 0)
    zero = jnp.zeros_like(qt)
    q2t = jnp.concatenate([jnp.where(feat < DK_A, qt, zero), jnp.where(feat >= DK_A, qt, zero)], axis=1)
    m_ref[...] = jnp.full(m_ref.shape, NEG_INF, F32)
    l_ref[...] = jnp.zeros(l_ref.shape, F32)
    acc_ref[...] = jnp.zeros(acc_ref.shape, F32)

    def block(j, masked):
        k_blk = jnp.concatenate([k_ref[j * r + c] for c in range(r)], axis=0)
        vt_blk = jnp.concatenate([vt_ref[j * r + c] for c in range(r)], axis=1)
        s = _dot(k_blk, q2t)
        if masked:
            key = lax.broadcasted_iota(jnp.int32, s.shape, 0) + j * t
            qp = lax.broadcasted_iota(jnp.int32, s.shape, 1)
            qp = jnp.where(qp >= t, qp - t, qp) + i * t
            s = jnp.where(key <= qp, s, NEG_INF)
        m_prev = m_ref[...]
        m_new = jnp.maximum(m_prev, jnp.max(s, axis=0, keepdims=True))
        alpha = jnp.exp(m_prev - m_new)
        p = jnp.exp(s - m_new)
        l_ref[...] = alpha * l_ref[...] + jnp.sum(p, axis=0, keepdims=True)
        acc_ref[...] = alpha * acc_ref[...] + _dot(vt_blk, p.astype(BF16))
        m_ref[...] = m_new

    def full_block(j, carry):
        block(j, False)
        return carry

    lax.fori_loop(0, i, full_block, 0)
    block(i, True)
    acc = acc_ref[...]
    l = l_ref[...]
    ot = acc[:, :t] / l[:, :t] - lam_ref[...] * (acc[:, t:] / l[:, t:])
    ot = ot * lax.rsqrt(jnp.mean(ot * ot, axis=0, keepdims=True) + EPS)
    o_ref[...] = ot.T * lnw_ref[...] * out_scale


def _prompt_attention(lam, qt, kab, vt, diff_ln_w, out_scale, b, l, t):
    tile = qt.shape[2]
    assert l % t == 0 and t % tile == 0
    nq = l // t
    r = t // tile
    nt = l // tile
    k3 = kab.reshape(b * nt, tile, SEG)
    out = pl.pallas_call(
        functools.partial(_pattn_body, t=t, r=r, out_scale=out_scale),
        grid=(b, H_A, nq),
        in_specs=[
            pl.BlockSpec((1, 1), lambda bi, h, i: (0, 0)),
            pl.BlockSpec((r, 128, tile), lambda bi, h, i: (bi * nq + i, h, 0)),
            pl.BlockSpec((nt, tile, 128), lambda bi, h, i: (bi, 0, h)),
            pl.BlockSpec((nt, 128, tile), lambda bi, h, i: (bi, h, 0)),
            pl.BlockSpec((1, DV_A), lambda bi, h, i: (0, 0)),
        ],
        out_specs=pl.BlockSpec((None, t, 128), lambda bi, h, i: (bi, i, h)),
        out_shape=jax.ShapeDtypeStruct((b, l, SEG), F32),
        scratch_shapes=[pltpu.VMEM((1, 2 * t), F32), pltpu.VMEM((1, 2 * t), F32), pltpu.VMEM((128, 2 * t), F32)],
        compiler_params=_cparams("parallel", "parallel", "arbitrary"),
        name="prompt_attn",
    )(lam, qt, k3, vt, diff_ln_w.reshape(1, DV_A))
    return out.reshape(b * l, SEG)


def _sattn_body(pt_ref, lam_ref, q_ref, *refs, n_grp, ls, out_scale):
    k_refs = refs[:n_grp]
    v_refs = refs[n_grp:2 * n_grp]
    kn_ref, vn_ref, lnw_ref, o_ref, q2_ref, m_ref, l_ref, acc_ref = refs[2 * n_grp:]
    j = pl.program_id(1)
    rows = 2 * ls

    @pl.when(j == 0)
    def _():
        q = q_ref[...]
        for h in range(H_A):
            q2_ref[h * rows:(h + 1) * rows, :] = _split_maps(q[:, h * 128:(h + 1) * 128])
        m_ref[...] = jnp.full(m_ref.shape, NEG_INF, F32)
        l_ref[...] = jnp.zeros(l_ref.shape, F32)
        acc_ref[...] = jnp.zeros(acc_ref.shape, F32)

    def head_update(h, kt, v, mask):
        hs = slice(h * rows, (h + 1) * rows)
        s = _dot(q2_ref[hs, :], kt)
        if mask is not None:
            s = jnp.where(mask, s, NEG_INF)
        _online_update(s, v, m_ref.at[hs, :], l_ref.at[hs, :], acc_ref.at[hs, :])

    for h in range(H_A):
        hl = slice(h * 128, (h + 1) * 128)
        head_update(h, jnp.concatenate([r[hl, :].astype(BF16) for r in k_refs], axis=1),
                    jnp.concatenate([r[pl.ds(h, PAGE_SIZE, stride=H_A), :].astype(BF16) for r in v_refs], axis=0),
                    None)

    @pl.when(j == pl.num_programs(1) - 1)
    def _():
        knt = kn_ref[...]
        vn = vn_ref[...]
        r = lax.broadcasted_iota(jnp.int32, (rows, PAGE_SIZE), 0)
        r = jnp.where(r >= ls, r - ls, r)
        c = lax.broadcasted_iota(jnp.int32, (rows, PAGE_SIZE), 1)
        mask = c <= r
        for h in range(H_A):
            hl = slice(h * 128, (h + 1) * 128)
            hs = slice(h * rows, (h + 1) * rows)
            head_update(h, knt[hl, :], vn[:, hl], mask)
            o_ref[:, hl] = _diff_finish(acc_ref[hs, :], l_ref[hs, :], ls, lam_ref[...], lnw_ref[...], out_scale)


def _sample_attention(lam, qab, kab, vab, cache_k, cache_v, page_table, diff_ln_w, out_scale, db, ls, n_grp):
    n_pages = page_table.shape[1]
    assert n_pages % n_grp == 0
    n_pool = cache_k.shape[0]
    ck = jnp.transpose(cache_k, (0, 2, 3, 4, 1)).reshape(n_pool, SEG, PAGE_SIZE)
    cv = cache_v.reshape(n_pool, PAGE_SIZE * H_A, DV_A)
    q3 = qab.reshape(db, ls, SEG)
    kn = jnp.pad(jnp.transpose(kab.reshape(db, ls, SEG), (0, 2, 1)), ((0, 0), (0, 0), (0, PAGE_SIZE - ls)))
    vn = jnp.pad(vab.reshape(db, ls, SEG), ((0, 0), (0, PAGE_SIZE - ls), (0, 0)))

    def page_spec(g, shape):
        return pl.BlockSpec((None,) + shape, lambda bi, j, pt: (pt[bi * n_pages + j * n_grp + g], 0, 0))

    per_b = pl.BlockSpec((None, ls, SEG), lambda bi, j, pt: (bi, 0, 0))
    rows = 2 * ls * H_A
    grid_spec = pltpu.PrefetchScalarGridSpec(
        num_scalar_prefetch=1,
        grid=(db, n_pages // n_grp),
        in_specs=[pl.BlockSpec((1, 1), lambda bi, j, pt: (0, 0)), per_b]
        + [page_spec(g, (SEG, PAGE_SIZE)) for g in range(n_grp)]
        + [page_spec(g, (PAGE_SIZE * H_A, DV_A)) for g in range(n_grp)]
        + [pl.BlockSpec((None, SEG, PAGE_SIZE), lambda bi, j, pt: (bi, 0, 0)),
           pl.BlockSpec((None, PAGE_SIZE, SEG), lambda bi, j, pt: (bi, 0, 0)),
           pl.BlockSpec((1, DV_A), lambda bi, j, pt: (0, 0))],
        out_specs=per_b,
        scratch_shapes=[pltpu.VMEM((rows, 128), BF16), pltpu.VMEM((rows, 1), F32),
                        pltpu.VMEM((rows, 1), F32), pltpu.VMEM((rows, 128), F32)],
    )
    out = pl.pallas_call(
        functools.partial(_sattn_body, n_grp=n_grp, ls=ls, out_scale=out_scale),
        grid_spec=grid_spec,
        out_shape=jax.ShapeDtypeStruct((db, ls, SEG), F32),
        compiler_params=_cparams("parallel", "arbitrary"),
        name="sample_attn",
    )(page_table.reshape(-1), lam, q3, *([ck] * n_grp), *([cv] * n_grp), kn, vn, diff_ln_w.reshape(1, DV_A))
    return out.reshape(db * ls, SEG)


def _hgrn_body(q_ref, k_ref, v_ref, lf_ref, g_ref, s0_ref, nw_ref, o_ref, sout_ref, st_ref, b_ref,
               *, chunk, sub, n_chunks):
    t = pl.program_id(1)

    @pl.when(t == 0)
    def _():
        for h in range(H_B):
            st_ref[h] = s0_ref[h].T

    r = lax.broadcasted_iota(jnp.int32, (chunk, chunk), 0)
    c = lax.broadcasted_iota(jnp.int32, (chunk, chunk), 1)
    tri = (c <= r).astype(F32)
    nw = nw_ref[...]

    n_sub = chunk // sub
    heads = range(H_B)
    hl = [slice(h * DK_B, (h + 1) * DK_B) for h in heads]

    def one_chunk(ci, carry):
        rows = pl.ds(pl.multiple_of(ci * chunk, chunk), chunk)
        b_ref[...] = jnp.dot(tri, lf_ref[rows, :], precision=lax.Precision.HIGHEST, preferred_element_type=F32)
        q = [q_ref[rows, hl[h]] for h in heads]
        k = [k_ref[rows, hl[h]] for h in heads]
        v = [v_ref[rows, hl[h]] for h in heads]
        b = [b_ref[:, hl[h]] for h in heads]
        st = [st_ref[h] for h in heads]
        att = {}
        for h in heads:
            for j in range(n_sub):
                ref = b_ref[j * sub - 1:j * sub, hl[h]] if j > 0 else jnp.zeros((1, DK_B), F32)
                ncol = (j + 1) * sub
                qj = (q[h][j * sub:ncol] * jnp.exp(b[h][j * sub:ncol] - ref)).astype(BF16)
                kj = (k[h][:ncol] * jnp.exp(ref - b[h][:ncol])).astype(BF16)
                att[h, j] = _dot_nt(qj, kj)
        inter = [_dot_nt((q[h] * jnp.exp(b[h])).astype(BF16), st[h].astype(BF16)) for h in heads]
        b_last = [b[h][chunk - 1:chunk, :] for h in heads]
        upd = [_dot(v[h].T, (k[h] * jnp.exp(b_last[h] - b[h])).astype(BF16)) for h in heads]
        for h in heads:
            o_parts = []
            for j in range(n_sub):
                ncol = (j + 1) * sub
                causal = (lax.broadcasted_iota(jnp.int32, (sub, ncol), 1)
                          <= lax.broadcasted_iota(jnp.int32, (sub, ncol), 0) + j * sub)
                a = jnp.where(causal, att[h, j], 0.0)
                o_parts.append(_dot(a.astype(BF16), v[h][:ncol]))
            o = jnp.concatenate(o_parts, axis=0) + inter[h]
            st_ref[h] = st[h] * jnp.exp(b_last[h]) + upd[h]
            gate = g_ref[rows, hl[h]]
            o_ref[rows, hl[h]] = _rms(o, nw) * (gate * jax.nn.sigmoid(gate))
        return carry

    lax.fori_loop(0, n_chunks, one_chunk, 0)

    @pl.when(t == pl.num_programs(1) - 1)
    def _():
        for h in range(H_B):
            sout_ref[h] = st_ref[h].T


def _hgrn(qb, kb, ib, lf, gb, s0, norm_w, b, l, chunk, n_chunks):
    rows = chunk * n_chunks
    assert l % rows == 0
    sub = min(HGRN_SUB, chunk)
    a3 = lambda t: t.reshape(b, l, SEG)
    tok = pl.BlockSpec((None, rows, SEG), lambda bi, t: (bi, t, 0))
    st = pl.BlockSpec((None, H_B, DK_B, DV_B), lambda bi, t: (bi, 0, 0, 0))
    out, s_fin = pl.pallas_call(
        functools.partial(_hgrn_body, chunk=chunk, sub=sub, n_chunks=n_chunks),
        grid=(b, l // rows),
        in_specs=[tok, tok, tok, tok, tok, st, pl.BlockSpec((1, DV_B), lambda bi, t: (0, 0))],
        out_specs=[tok, st],
        out_shape=[jax.ShapeDtypeStruct((b, l, SEG), F32), jax.ShapeDtypeStruct((b, H_B, DK_B, DV_B), F32)],
        scratch_shapes=[pltpu.VMEM((H_B, DV_B, DK_B), F32), pltpu.VMEM((chunk, SEG), F32)],
        compiler_params=_cparams("parallel", "arbitrary"),
        name="hgrn2",
    )(a3(qb), a3(kb), a3(ib), a3(lf), a3(gb), s0, norm_w.reshape(1, DV_B))
    return out.reshape(b * l, SEG), s_fin


def _mix_body(x_ref, oa_ref, ob_ref, w_ref, o_ref):
    o_ref[...] = (x_ref[...] + _dot(oa_ref[...].astype(BF16), w_ref[:SEG, :])
                  + _dot(ob_ref[...].astype(BF16), w_ref[SEG:, :]))


def _mix(x2d, oa, ob, w_out_bf, tm):
    n = x2d.shape[0]
    row = lambda i: (i, 0)
    return pl.pallas_call(
        _mix_body,
        grid=(n // tm,),
        in_specs=[pl.BlockSpec((tm, D_MODEL), row), pl.BlockSpec((tm, SEG), row), pl.BlockSpec((tm, SEG), row),
                  pl.BlockSpec((D_MODEL, D_MODEL), lambda i: (0, 0))],
        out_specs=pl.BlockSpec((tm, D_MODEL), row),
        out_shape=jax.ShapeDtypeStruct((n, D_MODEL), F32),
        compiler_params=_cparams("parallel"),
        name="mix_out",
    )(x2d, oa, ob, w_out_bf)


def _memkv_body(m_ref, nw_ref, wk_ref, wv_ref, k_ref, v_ref):
    hb = _rms(m_ref[...], nw_ref[...]).astype(BF16)
    k_ref[...] = _dot(hb, wk_ref[...])
    v_ref[...] = _dot(hb, wv_ref[...])


def _mem_kv(mem2d, norm_w, wk_bf, wv_bf, tm):
    n = mem2d.shape[0]
    row = lambda i: (i, 0)
    const = lambda i: (0, 0)
    o = jax.ShapeDtypeStruct((n, D_MODEL), F32)
    return pl.pallas_call(
        _memkv_body,
        grid=(n // tm,),
        in_specs=[pl.BlockSpec((tm, D_MODEL), row), pl.BlockSpec((1, D_MODEL), const),
                  pl.BlockSpec((D_MODEL, D_MODEL), const), pl.BlockSpec((D_MODEL, D_MODEL), const)],
        out_specs=[pl.BlockSpec((tm, D_MODEL), row)] * 2,
        out_shape=[o, o],
        compiler_params=_cparams("parallel"),
        name="mem_kv",
    )(mem2d, norm_w.reshape(1, D_MODEL), wk_bf, wv_bf)


def _memattn_body(x_ref, nw_ref, wq_ref, mk_ref, mv_ref, wo_ref, o_ref, *, head_rows):
    x = x_ref[...]
    q = _dot(_rms(x, nw_ref[...]).astype(BF16), wq_ref[...])
    q = (q * (DH_M ** -0.5)).astype(BF16)
    heads = []

    def head_of(ref, h):
        halves = [ref[pl.ds(half * H_M + h, MEM_LEN, stride=2 * H_M), :] for half in range(2)]
        return jnp.concatenate(halves, axis=1).astype(BF16)

    for h in range(H_M):
        hl = slice(h * DH_M, (h + 1) * DH_M)
        if head_rows:
            mk, mv = head_of(mk_ref, h), head_of(mv_ref, h)
        else:
            mk, mv = mk_ref[:, hl].astype(BF16), mv_ref[:, hl].astype(BF16)
        s = _dot_nt(q[:, hl], mk)
        p = jnp.exp(s - jnp.max(s, axis=-1, keepdims=True))
        o = _dot(p.astype(BF16), mv)
        heads.append(o / jnp.sum(p, axis=-1, keepdims=True))
    o_ref[...] = x + _dot(jnp.concatenate(heads, axis=1).astype(BF16), wo_ref[...])


def _mem_attend(x2d, norm_w, wq_bf, mk, mv, wo_bf, tm, tiles_per_batch):
    n = x2d.shape[0]
    row = lambda i: (i, 0)
    const = lambda i: (0, 0)
    head_rows = mk.shape[1] == MEM_LEN * 2 * H_M
    mem = pl.BlockSpec((None,) + mk.shape[1:], lambda i: (i // tiles_per_batch, 0, 0))
    return pl.pallas_call(
        functools.partial(_memattn_body, head_rows=head_rows),
        grid=(n // tm,),
        in_specs=[pl.BlockSpec((tm, D_MODEL), row), pl.BlockSpec((1, D_MODEL), const),
                  pl.BlockSpec((D_MODEL, D_MODEL), const), mem, mem, pl.BlockSpec((D_MODEL, D_MODEL), const)],
        out_specs=pl.BlockSpec((tm, D_MODEL), row),
        out_shape=jax.ShapeDtypeStruct((n, D_MODEL), F32),
        compiler_params=_cparams("parallel"),
        name="mem_attn",
    )(x2d, norm_w.reshape(1, D_MODEL), wq_bf, mk, mv, wo_bf)


_REMOVED_EXP = 100
_REMOVED_BITS = ((_REMOVED_EXP + 127) << 23) - (1 << 31)


def _extract_topk(work_ref, rank_ref, vals_ref, n_rows, k, exact):
    shape = work_ref.shape
    row = lax.broadcasted_iota(jnp.int32, shape, 0)

    def body(a, carry):
        w = work_ref[...]
        m = jnp.max(w, axis=0, keepdims=True)
        if exact:
            hit = row == jnp.min(jnp.where(w == m, row, n_rows), axis=0, keepdims=True)
        else:
            hit = w == m
        marker = pltpu.bitcast(jnp.full(shape, _REMOVED_BITS, jnp.int32) + (a << 23), F32)
        work_ref[...] = jnp.where(hit, marker, w)
        vals_ref[pl.ds(a, 1), :] = m
        return carry

    lax.fori_loop(0, k, body, 0)
    bits = pltpu.bitcast(work_ref[...], jnp.int32)
    order = ((bits >> 23) & 0xFF) - (_REMOVED_EXP + 127)
    removed = work_ref[...] <= -(2.0 ** _REMOVED_EXP)
    rank = jnp.where(removed, order, k).astype(F32)
    rank_ref[...] = rank
    n_removed = jnp.sum((rank < float(k)).astype(F32), axis=0, keepdims=True)
    return (n_removed > float(k)).astype(F32)


def _peer_select_body(x_ref, nw_ref, wq_ref, keys_ref, xn_ref, r1_ref, n2_ref, e1_ref, e2_ref, qt_ref,
                      work_ref, rank1_ref, rank2_ref, vals1_ref, vals2_ref, cand_ref, crank_ref, cvals_ref):
    hbt = _rms(x_ref[...], nw_ref[...]).T.astype(BF16)
    xn_ref[...] = hbt
    qt_ref[...] = _dot(wq_ref[...], hbt).astype(BF16)
    k = PEER_TOPK

    def select_head(h, exact):
        tie = jnp.zeros((1, qt_ref.shape[1]), F32)
        e_half = []
        for p, (rank_ref, vals_ref) in enumerate(((rank1_ref, vals1_ref), (rank2_ref, vals2_ref))):
            hp = 2 * h + p
            s = _dot(keys_ref[hp], qt_ref[hp * PEER_HALF:(hp + 1) * PEER_HALF, :])
            e_half.append(jnp.exp(s - jnp.max(s, axis=0, keepdims=True)))
            work_ref[...] = s
            tie = jnp.maximum(tie, _extract_topk(work_ref, rank_ref, vals_ref, N_KEYS, k, exact))
        v1 = vals1_ref[...]
        v2 = vals2_ref[...]
        cand_ref[...] = jnp.full(cand_ref.shape, NEG_INF, F32)
        for a, (off, nb_a) in enumerate(_CAND_ROWS):
            cand_ref[off:off + nb_a, :] = v1[a:a + 1, :] + v2[:nb_a, :]
        tie = jnp.maximum(tie, _extract_topk(cand_ref, crank_ref, cvals_ref, _N_CAND_PAD, k, exact))
        cv = cvals_ref[...]
        z = jnp.sum(jnp.exp(cv - cv[0:1, :]), axis=0, keepdims=True)
        sel = (crank_ref[...] < float(k)).astype(F32)
        tt = sel.shape[1]
        nb = None
        for off, nb_a in _CAND_ROWS:
            rows = sel[off:off + nb_a, :]
            if nb_a < k:
                rows = jnp.concatenate([rows, jnp.zeros((k - nb_a, tt), F32)], axis=0)
            nb = rows if nb is None else nb + rows
        rank2 = rank2_ref[...]
        n2 = jnp.zeros(rank2.shape, F32)
        for b in range(k):
            n2 = jnp.where(rank2 == float(b), nb[b:b + 1, :], n2)
        r1_ref[h] = rank1_ref[...]
        n2_ref[h] = n2.astype(BF16)
        e1_ref[h] = e_half[0]
        e2_ref[h] = (e_half[1] / z).astype(BF16)
        return tie

    for h in range(PEER_HEADS):
        tie = select_head(h, False)

        @pl.when(jnp.max(tie) > 0.0)
        def _():
            select_head(h, True)


def _cand_rows(k):
    rows, off = [], 0
    for a in range(k):
        nb_a = k // (a + 1)
        rows.append((off, nb_a))
        off += nb_a
    return tuple(rows), off


_CAND_ROWS, _N_CAND = _cand_rows(PEER_TOPK)
_N_CAND_PAD = -(-_N_CAND // 8) * 8


def _peer_select(x2d, norm_w, wq_bf, keys_bf, tt):
    n = x2d.shape[0]
    k = PEER_TOPK
    const2 = lambda i: (0, 0)
    tokmajor = pl.BlockSpec((PEER_HEADS, N_KEYS, tt), lambda i: (0, 0, i))
    kt = jax.ShapeDtypeStruct((PEER_HEADS, N_KEYS, n), F32)
    kt16 = jax.ShapeDtypeStruct((PEER_HEADS, N_KEYS, n), BF16)
    return pl.pallas_call(
        _peer_select_body,
        grid=(n // tt,),
        in_specs=[pl.BlockSpec((tt, D_MODEL), lambda i: (i, 0)), pl.BlockSpec((1, D_MODEL), const2),
                  pl.BlockSpec(wq_bf.shape, const2), pl.BlockSpec(keys_bf.shape, lambda i: (0, 0, 0))],
        out_specs=[pl.BlockSpec((D_MODEL, tt), lambda i: (0, i)), tokmajor, tokmajor, tokmajor, tokmajor],
        out_shape=[jax.ShapeDtypeStruct((D_MODEL, n), BF16), kt, kt16, kt, kt16],
        scratch_shapes=[pltpu.VMEM((2 * PEER_HEADS * PEER_HALF, tt), BF16),
                        pltpu.VMEM((N_KEYS, tt), F32), pltpu.VMEM((N_KEYS, tt), F32), pltpu.VMEM((N_KEYS, tt), F32),
                        pltpu.VMEM((k, tt), F32), pltpu.VMEM((k, tt), F32),
                        pltpu.VMEM((_N_CAND_PAD, tt), F32), pltpu.VMEM((_N_CAND_PAD, tt), F32),
                        pltpu.VMEM((k, tt), F32)],
        compiler_params=_cparams("parallel"),
        name="peer_select",
    )(x2d, norm_w.reshape(1, D_MODEL), wq_bf, keys_bf)


def _peer_dense_body(x_ref, xnt_ref, r1_ref, n2_ref, e1_ref, e2_ref, u_ref, vt_ref, fw_ref, o_ref,
                     yt_ref, ht_ref, pre0_ref, pre1_ref, *, blocks, nc, final_norm):
    s = pl.program_id(0)
    tt = xnt_ref.shape[1]
    reps = N_KEYS // _ROW_BCAST

    @pl.when(s == 0)
    def _():
        pre1_ref[...] = jnp.zeros(pre1_ref.shape, F32)
        yt_ref[...] = jnp.zeros(yt_ref.shape, F32)

    @pl.when(jnp.logical_and(s >= 1, (s - 1) % nc == 0))
    def _():
        yt_ref[...] = jnp.zeros(yt_ref.shape, F32)

    def bcast_row(ref, h, g):
        row = jnp.broadcast_to(ref[h, g:g + 1, :], (_ROW_BCAST, tt)).astype(BF16)
        return jnp.concatenate([row] * reps, axis=0)

    def step(pre_cur, pre_next):
        pre = pre_cur[...]
        act = (0.5 * pre * (1.0 + lax.erf(pre * (2.0 ** -0.5)))).astype(BF16)
        for g in range(blocks):
            w = None
            for h in range(PEER_HEADS):
                r1 = bcast_row(r1_ref, h, g)
                gate = bcast_row(e1_ref, h, g) * e2_ref[h]
                term = jnp.where(r1 < n2_ref[h], gate, jnp.zeros_like(gate))
                w = term if w is None else w + term
            ht_ref[g * N_KEYS:(g + 1) * N_KEYS, :] = w * act[g * N_KEYS:(g + 1) * N_KEYS, :]
        yt_ref[...] += _dot(vt_ref[...], ht_ref[...])
        pre_next[...] = _dot(u_ref[...], xnt_ref[...])

    @pl.when(s % 2 == 0)
    def _():
        step(pre1_ref, pre0_ref)

    @pl.when(s % 2 == 1)
    def _():
        step(pre0_ref, pre1_ref)

    @pl.when(jnp.logical_and(s >= 1, (s - 1) % nc == nc - 1))
    def _():
        y = x_ref[...] + yt_ref[...].T
        o_ref[...] = _rms(y, fw_ref[...]) if final_norm else y


_ROW_BCAST = 16


def _peer_dense(x2d, xnt, r1, n2, e1, e2, u_bf, vt_bf, final_w, tt, blocks, final_norm):
    n = x2d.shape[0]
    n_exp = u_bf.shape[0]
    ec = blocks * N_KEYS
    assert blocks % 8 == 0 and n_exp == N_KEYS * N_KEYS
    nc = n_exp // ec
    n_pairs = (n // tt) * nc

    def lagged(lag):
        def pair(s):
            p = jnp.clip(s - lag, 0, n_pairs - 1)
            return p // nc, p % nc
        return pair

    ahead, now = lagged(0), lagged(1)
    tok = pl.BlockSpec((tt, D_MODEL), lambda s: (now(s)[0], 0))
    rows = pl.BlockSpec((PEER_HEADS, blocks, tt), lambda s: (0, now(s)[1], now(s)[0]))
    tiles = pl.BlockSpec((PEER_HEADS, N_KEYS, tt), lambda s: (0, 0, now(s)[0]))
    return pl.pallas_call(
        functools.partial(_peer_dense_body, blocks=blocks, nc=nc, final_norm=final_norm),
        grid=(n_pairs + 1,),
        in_specs=[tok, pl.BlockSpec((D_MODEL, tt), lambda s: (0, ahead(s)[0])), rows, tiles, rows, tiles,
                  pl.BlockSpec((ec, D_MODEL), lambda s: (ahead(s)[1], 0)),
                  pl.BlockSpec((D_MODEL, ec), lambda s: (0, now(s)[1])),
                  pl.BlockSpec((1, D_MODEL), lambda s: (0, 0))],
        out_specs=tok,
        out_shape=jax.ShapeDtypeStruct((n, D_MODEL), F32),
        scratch_shapes=[pltpu.VMEM((D_MODEL, tt), F32), pltpu.VMEM((ec, tt), BF16),
                        pltpu.VMEM((ec, tt), F32), pltpu.VMEM((ec, tt), F32)],
        compiler_params=_cparams("arbitrary"),
        name="peer_dense",
    )(x2d, xnt, r1, n2, e1, e2, u_bf, vt_bf, final_w.reshape(1, D_MODEL))


def _peer(x2d, norm_w, wqt_bf, keys_bf, u_bf, vt_bf, final_w, final_norm, t_sel, t_dense, blocks):
    xnt, r1, n2, e1, e2 = _peer_select(x2d, norm_w, wqt_bf, keys_bf, t_sel)
    return _peer_dense(x2d, xnt, r1, n2, e1, e2, u_bf, vt_bf, final_w, t_dense, blocks, final_norm)


def _tile(n, want):
    return min(n, want)


def kernel(x_prompt, x_sample, cache_attn_k, cache_attn_v, cache_mem_k, cache_mem_v, state_hgrn, page_table, mem_prompt, norm_mix_w, w_in, lambda_q1, lambda_k1, lambda_q2, lambda_k2, diff_ln_w, hgrn_lower_bounds, hgrn_norm_w, w_out, norm_mem_q_w, norm_mem_kv_w, w_mq, w_mk, w_mv, w_mo, norm_ffn_w, peer_wq, peer_keys, peer_u, peer_v, final_norm_w):
    b, l = x_prompt.shape[:2]
    db, ls = x_sample.shape[:2]
    depth = w_in.shape[0]
    past = page_table.shape[1] * PAGE_SIZE
    pos_p = jnp.arange(l)
    tm_s = _tile(db * ls, 256)
    pos_s = past + (jnp.arange(tm_s) % ls)
    xp = x_prompt.reshape(b * l, D_MODEL)
    xs = x_sample.reshape(db * ls, D_MODEL)
    chunk_p = HGRN_CHUNK if l % HGRN_CHUNK == 0 else l
    chunk_s = HGRN_CHUNK if ls % HGRN_CHUNK == 0 else ls
    tq = _tile(l, 512)
    outs = [[] for _ in range(8)]
    for layer in range(depth):
        lam_init = 0.8 - 0.6 * math.exp(-0.3 * layer)
        lam = (jnp.exp(jnp.sum(lambda_q1[layer] * lambda_k1[layer]))
               - jnp.exp(jnp.sum(lambda_q2[layer] * lambda_k2[layer])) + lam_init).reshape(1, 1).astype(F32)
        last = layer == depth - 1
        w_in_bf = w_in[layer].astype(BF16)
        w_out_bf = w_out[layer].astype(BF16)
        wmq, wmk, wmv, wmo = (w[layer].astype(BF16) for w in (w_mq, w_mk, w_mv, w_mo))
        wqt_bf = peer_wq[layer].astype(BF16).T
        keys_bf = peer_keys[layer].astype(BF16).reshape(PEER_HEADS * 2, N_KEYS, PEER_HALF)
        u_bf = peer_u[layer].astype(BF16)
        vt_bf = peer_v[layer].astype(BF16).T
        peer = functools.partial(_peer, norm_w=norm_ffn_w[layer], wqt_bf=wqt_bf, keys_bf=keys_bf, u_bf=u_bf,
                                 vt_bf=vt_bf, final_w=final_norm_w, final_norm=last)

        kat, va, qat, kab, vat, qb, kb, ib, lf, gb = _inproj(
            xp, norm_mix_w[layer], w_in_bf, hgrn_lower_bounds, pos_p, layer, tq, True)
        oa = _prompt_attention(lam, qat, kab, vat, diff_ln_w[layer], 1.0 - lam_init, b, l, _tile(l, 1024))
        n_chunks = max(1, min(l, 512) // chunk_p)
        ob, sp = _hgrn(qb, kb, ib, lf, gb, jnp.zeros((b, H_B, DK_B, DV_B), F32), hgrn_norm_w[layer],
                       b, l, chunk_p, n_chunks)
        xp = _mix(xp, oa, ob, w_out_bf, _tile(l, 512))
        mk, mv = _mem_kv(mem_prompt.reshape(b * MEM_LEN, D_MODEL), norm_mem_kv_w[layer], wmk, wmv, MEM_LEN)
        tm = _tile(l, 512)
        xp = _mem_attend(xp, norm_mem_q_w[layer], wmq, mk.reshape(b, MEM_LEN, D_MODEL),
                         mv.reshape(b, MEM_LEN, D_MODEL), wmo, tm, l // tm)
        xp = peer(xp, t_sel=_tile(b * l, 512), t_dense=_tile(b * l, 512), blocks=8)
        outs[0].append(jnp.transpose(kat.reshape(b, H_A, 2, DK_A, l), (0, 4, 1, 2, 3)))
        outs[1].append(va.reshape(b, l, H_A, DV_A))
        outs[2].append(sp)
        outs[3].append(mk.reshape(b, MEM_LEN, H_M, DH_M))
        outs[4].append(mv.reshape(b, MEM_LEN, H_M, DH_M))

        ka, va, qab, kab, vab, qb, kb, ib, lf, gb = _inproj(
            xs, norm_mix_w[layer], w_in_bf, hgrn_lower_bounds, pos_s, layer, tm_s, False)
        oa = _sample_attention(lam, qab, kab, vab, cache_attn_k[layer], cache_attn_v[layer], page_table,
                               diff_ln_w[layer], 1.0 - lam_init, db, ls, min(32, page_table.shape[1]))
        ob, ss = _hgrn(qb, kb, ib, lf, gb, state_hgrn[layer], hgrn_norm_w[layer], db, ls, chunk_s, ls // chunk_s)
        xs = _mix(xs, oa, ob, w_out_bf, tm_s)
        mem_view = lambda c: jnp.transpose(c.reshape(db, MEM_LEN, H_M, 2, DH_M // 2),
                                           (0, 1, 3, 2, 4)).reshape(db, MEM_LEN * 2 * H_M, DH_M // 2)
        xs = _mem_attend(xs, norm_mem_q_w[layer], wmq, mem_view(cache_mem_k[layer]),
                         mem_view(cache_mem_v[layer]), wmo, ls, 1)
        xs = peer(xs, t_sel=_tile(db * ls, 512), t_dense=_tile(db * ls, 512), blocks=8)
        outs[5].append(ka.reshape(db, ls, H_A, 2, DK_A))
        outs[6].append(va.reshape(db, ls, H_A, DV_A))
        outs[7].append(ss)
    y_prompt = xp.reshape(b, l, D_MODEL)
    y_sample = xs.reshape(db, ls, D_MODEL)
    return (y_prompt, y_sample) + tuple(jnp.stack(o) for o in outs)
```

```python
import functools
import math

import jax
import jax.numpy as jnp
from jax import lax
from jax.experimental import pallas as pl
from jax.experimental.pallas import tpu as pltpu

F32 = jnp.float32
BF16 = jnp.bfloat16
EPS = 1e-6
NEG_INF = float("-inf")

D_MODEL = 1024
PAGE_SIZE = 128
H_A = 4
DV_A = 128
DK_A = 64
ROT_DIM = 16
ROPE_THETA = 500000.0
H_B = 4
DK_B = 128
DV_B = 128
HGRN_CHUNK = 64
HGRN_SUB = 16
MEM_LEN = 256
H_M = 4
DH_M = 256
N_KEYS = 128
PEER_HEADS = 8
PEER_TOPK = 16
PEER_HALF = 128
SEG = 512
N_SEG = 7

VMEM_LIMIT = 56 * 1024 * 1024


def _cparams(*sem, flags=None):
    return pltpu.CompilerParams(dimension_semantics=sem, vmem_limit_bytes=VMEM_LIMIT, flags=flags)


def _rms(x, w):
    return x * lax.rsqrt(jnp.mean(x * x, axis=-1, keepdims=True) + EPS) * w


def _dot_nt(a, b):
    return lax.dot_general(a, b, (((1,), (1,)), ((), ())), preferred_element_type=F32)


def _dot(a, b):
    return jnp.dot(a, b, preferred_element_type=F32)


def _inproj_body(x_ref, nw_ref, w_ref, lbp_ref, rc_ref, rs1_ref, rs2_ref,
                 ka_ref, va_ref, qab_ref, kab_ref, vab_ref,
                 qb_ref, kb_ref, ib_ref, lf_ref, gb_ref, *, layer, transposed):
    hb = _rms(x_ref[...], nw_ref[...]).astype(BF16)

    def seg(i):
        return _dot(hb, w_ref[:, i * SEG:(i + 1) * SEG])

    rc, rs1, rs2 = rc_ref[...], rs1_ref[...], rs2_ref[...]

    def rope_block(blk):
        return blk * rc + pltpu.roll(blk, 128 - ROT_DIM // 2, 1) * rs1 + pltpu.roll(blk, ROT_DIM // 2, 1) * rs2

    qa = seg(0)
    ka = seg(1)
    for g in range(SEG // 128):
        sl = slice(g * 128, (g + 1) * 128)
        qr = rope_block(qa[:, sl]) * (DK_A ** -0.5)
        if transposed:
            qab_ref[sl, :] = qr.T.astype(BF16)
        else:
            qab_ref[:, sl] = qr.astype(BF16)
        kr = rope_block(ka[:, sl])
        if transposed:
            ka_ref[sl, :] = kr.T
        else:
            ka_ref[:, sl] = kr
        kab_ref[:, sl] = kr.astype(BF16)
    va = seg(2)
    va_ref[...] = va
    if transposed:
        for g in range(SEG // 128):
            sl = slice(g * 128, (g + 1) * 128)
            vab_ref[sl, :] = va[:, sl].T.astype(BF16)
    else:
        vab_ref[...] = va.astype(BF16)
    qb_ref[...] = seg(3)
    lbp = lbp_ref[...]
    e = jnp.exp(lbp - jnp.max(lbp, axis=0, keepdims=True))
    lb = jnp.sum(e[:layer + 1], axis=0, keepdims=True) / jnp.sum(e, axis=0, keepdims=True)
    fg = lb + (1.0 - lb) * jax.nn.sigmoid(seg(4))
    lf_ref[...] = jnp.log(fg)
    kb_ref[...] = 1.0 - fg
    ib_ref[...] = seg(5).astype(BF16)
    gb_ref[...] = seg(6)


def _rope_tables(pos):
    half = ROT_DIM // 2
    inv = ROPE_THETA ** (-jnp.arange(half, dtype=F32) * 2.0 / ROT_DIM)
    ang = pos.astype(F32)[:, None] * inv[None, :]
    cos, sin = jnp.cos(ang), jnp.sin(ang)
    p = pos.shape[0]
    ones = jnp.ones((p, DK_A - ROT_DIM), F32)
    zeros = jnp.zeros((p, DK_A - ROT_DIM), F32)
    zh = jnp.zeros((p, half), F32)
    rc = jnp.concatenate([cos, cos, ones], axis=1)
    rs1 = jnp.concatenate([-sin, zh, zeros], axis=1)
    rs2 = jnp.concatenate([zh, sin, zeros], axis=1)
    return tuple(jnp.concatenate([t, t], axis=1) for t in (rc, rs1, rs2))


def _inproj(x2d, norm_w, w_in_bf, lower_bounds, pos, layer, tm, transposed):
    n = x2d.shape[0]
    p = pos.shape[0]
    assert n % tm == 0 and p % tm == 0
    npb = p // tm
    rc, rs1, rs2 = _rope_tables(pos)
    row = lambda i: (i, 0)
    const = lambda i: (0, 0)
    tab = lambda i: (i % npb, 0)
    f32o = jax.ShapeDtypeStruct((n, SEG), F32)
    bf16o = jax.ShapeDtypeStruct((n, SEG), BF16)
    out_spec = pl.BlockSpec((tm, SEG), row)
    out_specs = [out_spec] * 10
    out_shape = [f32o, f32o, bf16o, bf16o, bf16o, f32o, f32o, bf16o, f32o, f32o]
    if transposed:
        t_spec = pl.BlockSpec((None, SEG, tm), lambda i: (i, 0, 0))
        t_shape = jax.ShapeDtypeStruct((n // tm, SEG, tm), BF16)
        out_specs[2] = out_specs[4] = t_spec
        out_shape[2] = out_shape[4] = t_shape
        out_specs[0] = pl.BlockSpec((None, SEG, tm), lambda i: (i // npb, 0, i % npb))
        out_shape[0] = jax.ShapeDtypeStruct((n // p, SEG, p), F32)
    return pl.pallas_call(
        functools.partial(_inproj_body, layer=layer, transposed=transposed),
        grid=(n // tm,),
        in_specs=[
            pl.BlockSpec((tm, D_MODEL), row),
            pl.BlockSpec((1, D_MODEL), const),
            pl.BlockSpec((D_MODEL, N_SEG * SEG), const),
            pl.BlockSpec(lower_bounds.shape, const),
            pl.BlockSpec((tm, 128), tab),
            pl.BlockSpec((tm, 128), tab),
            pl.BlockSpec((tm, 128), tab),
        ],
        out_specs=out_specs,
        out_shape=out_shape,
        compiler_params=_cparams("parallel"),
        name="inproj",
    )(x2d, norm_w.reshape(1, D_MODEL), w_in_bf, lower_bounds, rc, rs1, rs2)


def _split_maps(q):
    lane = lax.broadcasted_iota(jnp.int32, q.shape, 1)
    zero = jnp.zeros_like(q)
    return jnp.concatenate([jnp.where(lane < DK_A, q, zero), jnp.where(lane >= DK_A, q, zero)], axis=0)


def _online_update(s, v, m_ref, l_ref, acc_ref):
    m_prev = m_ref[...]
    m_new = jnp.maximum(m_prev, jnp.max(s, axis=-1, keepdims=True))
    alpha = jnp.exp(m_prev - m_new)
    p = jnp.exp(s - m_new)
    l_ref[...] = alpha * l_ref[...] + jnp.sum(p, axis=-1, keepdims=True)
    acc_ref[...] = alpha * acc_ref[...] + _dot(p.astype(BF16), v)
    m_ref[...] = m_new


def _diff_finish(acc, l, t, lam, lnw, out_scale):
    o = acc[:t] / l[:t] - lam * (acc[t:] / l[t:])
    return _rms(o, lnw) * out_scale


def _pattn_body(lam_ref, qt_ref, k_ref, vt_ref, lnw_ref, o_ref, m_ref, l_ref, acc_ref, *, t, r, out_scale):
    i = pl.program_id(2)
    qt = jnp.concatenate([qt_ref[c] for c in range(r)], axis=1)
    feat = lax.broadcasted_iota(jnp.int32, qt.shape, 0)
    zero = jnp.zeros_like(qt)
    q2t = jnp.concatenate([jnp.where(feat < DK_A, qt, zero), jnp.where(feat >= DK_A, qt, zero)], axis=1)
    m_ref[...] = jnp.full(m_ref.shape, NEG_INF, F32)
    l_ref[...] = jnp.zeros(l_ref.shape, F32)
    acc_ref[...] = jnp.zeros(acc_ref.shape, F32)

    def block(j, masked):
        k_blk = jnp.concatenate([k_ref[j * r + c] for c in range(r)], axis=0)
        vt_blk = jnp.concatenate([vt_ref[j * r + c] for c in range(r)], axis=1)
        s = _dot(k_blk, q2t)
        if masked:
            key = lax.broadcasted_iota(jnp.int32, s.shape, 0) + j * t
            qp = lax.broadcasted_iota(jnp.int32, s.shape, 1)
            qp = jnp.where(qp >= t, qp - t, qp) + i * t
            s = jnp.where(key <= qp, s, NEG_INF)
        m_prev = m_ref[...]
        m_new = jnp.maximum(m_prev, jnp.max(s, axis=0, keepdims=True))
        alpha = jnp.exp(m_prev - m_new)
        p = jnp.exp(s - m_new)
        l_ref[...] = alpha * l_ref[...] + jnp.sum(p, axis=0, keepdims=True)
        acc_ref[...] = alpha * acc_ref[...] + _dot(vt_blk, p.astype(BF16))
        m_ref[...] = m_new

    def full_block(j, carry):
        block(j, False)
        return carry

    lax.fori_loop(0, i, full_block, 0)
    block(i, True)
    acc = acc_ref[...]
    l = l_ref[...]
    ot = acc[:, :t] / l[:, :t] - lam_ref[...] * (acc[:, t:] / l[:, t:])
    ot = ot * lax.rsqrt(jnp.mean(ot * ot, axis=0, keepdims=True) + EPS)
    o_ref[...] = ot.T * lnw_ref[...] * out_scale


def _prompt_attention(lam, qt, kab, vt, diff_ln_w, out_scale, b, l, t):
    tile = qt.shape[2]
    assert l % t == 0 and t % tile == 0
    nq = l // t
    r = t // tile
    nt = l // tile
    k3 = kab.reshape(b * nt, tile, SEG)
    out = pl.pallas_call(
        functools.partial(_pattn_body, t=t, r=r, out_scale=out_scale),
        grid=(b, H_A, nq),
        in_specs=[
            pl.BlockSpec((1, 1), lambda bi, h, i: (0, 0)),
            pl.BlockSpec((r, 128, tile), lambda bi, h, i: (bi * nq + i, h, 0)),
            pl.BlockSpec((nt, tile, 128), lambda bi, h, i: (bi, 0, h)),
            pl.BlockSpec((nt, 128, tile), lambda bi, h, i: (bi, h, 0)),
            pl.BlockSpec((1, DV_A), lambda bi, h, i: (0, 0)),
        ],
        out_specs=pl.BlockSpec((None, t, 128), lambda bi, h, i: (bi, i, h)),
        out_shape=jax.ShapeDtypeStruct((b, l, SEG), F32),
        scratch_shapes=[pltpu.VMEM((1, 2 * t), F32), pltpu.VMEM((1, 2 * t), F32), pltpu.VMEM((128, 2 * t), F32)],
        compiler_params=_cparams("parallel", "parallel", "arbitrary"),
        name="prompt_attn",
    )(lam, qt, k3, vt, diff_ln_w.reshape(1, DV_A))
    return out.reshape(b * l, SEG)


def _sattn_body(pt_ref, lam_ref, q_ref, *refs, n_grp, ls, out_scale):
    k_refs = refs[:n_grp]
    v_refs = refs[n_grp:2 * n_grp]
    kn_ref, vn_ref, lnw_ref, o_ref, q2_ref, m_ref, l_ref, acc_ref = refs[2 * n_grp:]
    j = pl.program_id(1)
    rows = 2 * ls

    @pl.when(j == 0)
    def _():
        q = q_ref[...]
        for h in range(H_A):
            q2_ref[h * rows:(h + 1) * rows, :] = _split_maps(q[:, h * 128:(h + 1) * 128])
        m_ref[...] = jnp.full(m_ref.shape, NEG_INF, F32)
        l_ref[...] = jnp.zeros(l_ref.shape, F32)
        acc_ref[...] = jnp.zeros(acc_ref.shape, F32)

    def head_update(h, kt, v, mask):
        hs = slice(h * rows, (h + 1) * rows)
        s = _dot(q2_ref[hs, :], kt)
        if mask is not None:
            s = jnp.where(mask, s, NEG_INF)
        _online_update(s, v, m_ref.at[hs, :], l_ref.at[hs, :], acc_ref.at[hs, :])

    for h in range(H_A):
        hl = slice(h * 128, (h + 1) * 128)
        head_update(h, jnp.concatenate([r[hl, :].astype(BF16) for r in k_refs], axis=1),
                    jnp.concatenate([r[pl.ds(h, PAGE_SIZE, stride=H_A), :].astype(BF16) for r in v_refs], axis=0),
                    None)

    @pl.when(j == pl.num_programs(1) - 1)
    def _():
        knt = kn_ref[...]
        vn = vn_ref[...]
        r = lax.broadcasted_iota(jnp.int32, (rows, PAGE_SIZE), 0)
        r = jnp.where(r >= ls, r - ls, r)
        c = lax.broadcasted_iota(jnp.int32, (rows, PAGE_SIZE), 1)
        mask = c <= r
        for h in range(H_A):
            hl = slice(h * 128, (h + 1) * 128)
            hs = slice(h * rows, (h + 1) * rows)
            head_update(h, knt[hl, :], vn[:, hl], mask)
            o_ref[:, hl] = _diff_finish(acc_ref[hs, :], l_ref[hs, :], ls, lam_ref[...], lnw_ref[...], out_scale)


def _sample_attention(lam, qab, kab, vab, cache_k, cache_v, page_table, diff_ln_w, out_scale, db, ls, n_grp):
    n_pages = page_table.shape[1]
    assert n_pages % n_grp == 0
    n_pool = cache_k.shape[0]
    ck = jnp.transpose(cache_k, (0, 2, 3, 4, 1)).reshape(n_pool, SEG, PAGE_SIZE)
    cv = cache_v.reshape(n_pool, PAGE_SIZE * H_A, DV_A)
    q3 = qab.reshape(db, ls, SEG)
    kn = jnp.pad(jnp.transpose(kab.reshape(db, ls, SEG), (0, 2, 1)), ((0, 0), (0, 0), (0, PAGE_SIZE - ls)))
    vn = jnp.pad(vab.reshape(db, ls, SEG), ((0, 0), (0, PAGE_SIZE - ls), (0, 0)))

    def page_spec(g, shape):
        return pl.BlockSpec((None,) + shape, lambda bi, j, pt: (pt[bi * n_pages + j * n_grp + g], 0, 0))

    per_b = pl.BlockSpec((None, ls, SEG), lambda bi, j, pt: (bi, 0, 0))
    rows = 2 * ls * H_A
    grid_spec = pltpu.PrefetchScalarGridSpec(
        num_scalar_prefetch=1,
        grid=(db, n_pages // n_grp),
        in_specs=[pl.BlockSpec((1, 1), lambda bi, j, pt: (0, 0)), per_b]
        + [page_spec(g, (SEG, PAGE_SIZE)) for g in range(n_grp)]
        + [page_spec(g, (PAGE_SIZE * H_A, DV_A)) for g in range(n_grp)]
        + [pl.BlockSpec((None, SEG, PAGE_SIZE), lambda bi, j, pt: (bi, 0, 0)),
           pl.BlockSpec((None, PAGE_SIZE, SEG), lambda bi, j, pt: (bi, 0, 0)),
           pl.BlockSpec((1, DV_A), lambda bi, j, pt: (0, 0))],
        out_specs=per_b,
        scratch_shapes=[pltpu.VMEM((rows, 128), BF16), pltpu.VMEM((rows, 1), F32),
                        pltpu.VMEM((rows, 1), F32), pltpu.VMEM((rows, 128), F32)],
    )
    out = pl.pallas_call(
        functools.partial(_sattn_body, n_grp=n_grp, ls=ls, out_scale=out_scale),
        grid_spec=grid_spec,
        out_shape=jax.ShapeDtypeStruct((db, ls, SEG), F32),
        compiler_params=_cparams("parallel", "arbitrary"),
        name="sample_attn",
    )(page_table.reshape(-1), lam, q3, *([ck] * n_grp), *([cv] * n_grp), kn, vn, diff_ln_w.reshape(1, DV_A))
    return out.reshape(db * ls, SEG)


def _hgrn_body(q_ref, k_ref, v_ref, lf_ref, g_ref, s0_ref, nw_ref, o_ref, sout_ref, st_ref, b_ref,
               *, chunk, sub, n_chunks):
    t = pl.program_id(1)

    @pl.when(t == 0)
    def _():
        for h in range(H_B):
            st_ref[h] = s0_ref[h].T

    r = lax.broadcasted_iota(jnp.int32, (chunk, chunk), 0)
    c = lax.broadcasted_iota(jnp.int32, (chunk, chunk), 1)
    tri = (c <= r).astype(F32)
    nw = nw_ref[...]

    n_sub = chunk // sub
    heads = range(H_B)
    hl = [slice(h * DK_B, (h + 1) * DK_B) for h in heads]

    def one_chunk(ci, carry):
        rows = pl.ds(pl.multiple_of(ci * chunk, chunk), chunk)
        b_ref[...] = jnp.dot(tri, lf_ref[rows, :], precision=lax.Precision.HIGHEST, preferred_element_type=F32)
        q = [q_ref[rows, hl[h]] for h in heads]
        k = [k_ref[rows, hl[h]] for h in heads]
        v = [v_ref[rows, hl[h]] for h in heads]
        b = [b_ref[:, hl[h]] for h in heads]
        st = [st_ref[h] for h in heads]
        att = {}
        for h in heads:
            for j in range(n_sub):
                ref = b_ref[j * sub - 1:j * sub, hl[h]] if j > 0 else jnp.zeros((1, DK_B), F32)
                ncol = (j + 1) * sub
                qj = (q[h][j * sub:ncol] * jnp.exp(b[h][j * sub:ncol] - ref)).astype(BF16)
                kj = (k[h][:ncol] * jnp.exp(ref - b[h][:ncol])).astype(BF16)
                att[h, j] = _dot_nt(qj, kj)
        inter = [_dot_nt((q[h] * jnp.exp(b[h])).astype(BF16), st[h].astype(BF16)) for h in heads]
        b_last = [b[h][chunk - 1:chunk, :] for h in heads]
        upd = [_dot(v[h].T, (k[h] * jnp.exp(b_last[h] - b[h])).astype(BF16)) for h in heads]
        for h in heads:
            o_parts = []
            for j in range(n_sub):
                ncol = (j + 1) * sub
                causal = (lax.broadcasted_iota(jnp.int32, (sub, ncol), 1)
                          <= lax.broadcasted_iota(jnp.int32, (sub, ncol), 0) + j * sub)
                a = jnp.where(causal, att[h, j], 0.0)
                o_parts.append(_dot(a.astype(BF16), v[h][:ncol]))
            o = jnp.concatenate(o_parts, axis=0) + inter[h]
            st_ref[h] = st[h] * jnp.exp(b_last[h]) + upd[h]
            gate = g_ref[rows, hl[h]]
            o_ref[rows, hl[h]] = _rms(o, nw) * (gate * jax.nn.sigmoid(gate))
        return carry

    lax.fori_loop(0, n_chunks, one_chunk, 0)

    @pl.when(t == pl.num_programs(1) - 1)
    def _():
        for h in range(H_B):
            sout_ref[h] = st_ref[h].T


def _hgrn(qb, kb, ib, lf, gb, s0, norm_w, b, l, chunk, n_chunks):
    rows = chunk * n_chunks
    assert l % rows == 0
    sub = min(HGRN_SUB, chunk)
    a3 = lambda t: t.reshape(b, l, SEG)
    tok = pl.BlockSpec((None, rows, SEG), lambda bi, t: (bi, t, 0))
    st = pl.BlockSpec((None, H_B, DK_B, DV_B), lambda bi, t: (bi, 0, 0, 0))
    out, s_fin = pl.pallas_call(
        functools.partial(_hgrn_body, chunk=chunk, sub=sub, n_chunks=n_chunks),
        grid=(b, l // rows),
        in_specs=[tok, tok, tok, tok, tok, st, pl.BlockSpec((1, DV_B), lambda bi, t: (0, 0))],
        out_specs=[tok, st],
        out_shape=[jax.ShapeDtypeStruct((b, l, SEG), F32), jax.ShapeDtypeStruct((b, H_B, DK_B, DV_B), F32)],
        scratch_shapes=[pltpu.VMEM((H_B, DV_B, DK_B), F32), pltpu.VMEM((chunk, SEG), F32)],
        compiler_params=_cparams("parallel", "arbitrary"),
        name="hgrn2",
    )(a3(qb), a3(kb), a3(ib), a3(lf), a3(gb), s0, norm_w.reshape(1, DV_B))
    return out.reshape(b * l, SEG), s_fin


def _mix_body(x_ref, oa_ref, ob_ref, w_ref, o_ref):
    o_ref[...] = (x_ref[...] + _dot(oa_ref[...].astype(BF16), w_ref[:SEG, :])
                  + _dot(ob_ref[...].astype(BF16), w_ref[SEG:, :]))


def _mix(x2d, oa, ob, w_out_bf, tm):
    n = x2d.shape[0]
    row = lambda i: (i, 0)
    return pl.pallas_call(
        _mix_body,
        grid=(n // tm,),
        in_specs=[pl.BlockSpec((tm, D_MODEL), row), pl.BlockSpec((tm, SEG), row), pl.BlockSpec((tm, SEG), row),
                  pl.BlockSpec((D_MODEL, D_MODEL), lambda i: (0, 0))],
        out_specs=pl.BlockSpec((tm, D_MODEL), row),
        out_shape=jax.ShapeDtypeStruct((n, D_MODEL), F32),
        compiler_params=_cparams("parallel"),
        name="mix_out",
    )(x2d, oa, ob, w_out_bf)


def _memkv_body(m_ref, nw_ref, wk_ref, wv_ref, k_ref, v_ref):
    hb = _rms(m_ref[...], nw_ref[...]).astype(BF16)
    k_ref[...] = _dot(hb, wk_ref[...])
    v_ref[...] = _dot(hb, wv_ref[...])


def _mem_kv(mem2d, norm_w, wk_bf, wv_bf, tm):
    n = mem2d.shape[0]
    row = lambda i: (i, 0)
    const = lambda i: (0, 0)
    o = jax.ShapeDtypeStruct((n, D_MODEL), F32)
    return pl.pallas_call(
        _memkv_body,
        grid=(n // tm,),
        in_specs=[pl.BlockSpec((tm, D_MODEL), row), pl.BlockSpec((1, D_MODEL), const),
                  pl.BlockSpec((D_MODEL, D_MODEL), const), pl.BlockSpec((D_MODEL, D_MODEL), const)],
        out_specs=[pl.BlockSpec((tm, D_MODEL), row)] * 2,
        out_shape=[o, o],
        compiler_params=_cparams("parallel"),
        name="mem_kv",
    )(mem2d, norm_w.reshape(1, D_MODEL), wk_bf, wv_bf)


def _memattn_body(x_ref, nw_ref, wq_ref, mk_ref, mv_ref, wo_ref, o_ref, *, head_rows):
    x = x_ref[...]
    q = _dot(_rms(x, nw_ref[...]).astype(BF16), wq_ref[...])
    q = (q * (DH_M ** -0.5)).astype(BF16)
    heads = []

    def head_of(ref, h):
        halves = [ref[pl.ds(half * H_M + h, MEM_LEN, stride=2 * H_M), :] for half in range(2)]
        return jnp.concatenate(halves, axis=1).astype(BF16)

    for h in range(H_M):
        hl = slice(h * DH_M, (h + 1) * DH_M)
        if head_rows:
            mk, mv = head_of(mk_ref, h), head_of(mv_ref, h)
        else:
            mk, mv = mk_ref[:, hl].astype(BF16), mv_ref[:, hl].astype(BF16)
        s = _dot_nt(q[:, hl], mk)
        p = jnp.exp(s - jnp.max(s, axis=-1, keepdims=True))
        o = _dot(p.astype(BF16), mv)
        heads.append(o / jnp.sum(p, axis=-1, keepdims=True))
    o_ref[...] = x + _dot(jnp.concatenate(heads, axis=1).astype(BF16), wo_ref[...])


def _mem_attend(x2d, norm_w, wq_bf, mk, mv, wo_bf, tm, tiles_per_batch):
    n = x2d.shape[0]
    row = lambda i: (i, 0)
    const = lambda i: (0, 0)
    head_rows = mk.shape[1] == MEM_LEN * 2 * H_M
    mem = pl.BlockSpec((None,) + mk.shape[1:], lambda i: (i // tiles_per_batch, 0, 0))
    return pl.pallas_call(
        functools.partial(_memattn_body, head_rows=head_rows),
        grid=(n // tm,),
        in_specs=[pl.BlockSpec((tm, D_MODEL), row), pl.BlockSpec((1, D_MODEL), const),
                  pl.BlockSpec((D_MODEL, D_MODEL), const), mem, mem, pl.BlockSpec((D_MODEL, D_MODEL), const)],
        out_specs=pl.BlockSpec((tm, D_MODEL), row),
        out_shape=jax.ShapeDtypeStruct((n, D_MODEL), F32),
        compiler_params=_cparams("parallel"),
        name="mem_attn",
    )(x2d, norm_w.reshape(1, D_MODEL), wq_bf, mk, mv, wo_bf)


_REMOVED_EXP = 100
_REMOVED_BITS = ((_REMOVED_EXP + 127) << 23) - (1 << 31)


def _extract_topk(work_ref, rank_ref, vals_ref, n_rows, k, exact):
    shape = work_ref.shape
    row = lax.broadcasted_iota(jnp.int32, shape, 0)

    def body(a, carry):
        w = work_ref[...]
        m = jnp.max(w, axis=0, keepdims=True)
        if exact:
            hit = row == jnp.min(jnp.where(w == m, row, n_rows), axis=0, keepdims=True)
        else:
            hit = w == m
        marker = pltpu.bitcast(jnp.full(shape, _REMOVED_BITS, jnp.int32) + (a << 23), F32)
        work_ref[...] = jnp.where(hit, marker, w)
        vals_ref[pl.ds(a, 1), :] = m
        return carry

    lax.fori_loop(0, k, body, 0)
    bits = pltpu.bitcast(work_ref[...], jnp.int32)
    order = ((bits >> 23) & 0xFF) - (_REMOVED_EXP + 127)
    removed = work_ref[...] <= -(2.0 ** _REMOVED_EXP)
    rank = jnp.where(removed, order, k).astype(F32)
    rank_ref[...] = rank
    n_removed = jnp.sum((rank < float(k)).astype(F32), axis=0, keepdims=True)
    return (n_removed > float(k)).astype(F32)


def _peer_select_body(x_ref, nw_ref, wq_ref, keys_ref, xn_ref, r1_ref, n2_ref, e1_ref, e2_ref, qt_ref,
                      work_ref, rank1_ref, rank2_ref, vals1_ref, vals2_ref, cand_ref, crank_ref, cvals_ref):
    hbt = _rms(x_ref[...], nw_ref[...]).T.astype(BF16)
    xn_ref[...] = hbt
    qt_ref[...] = _dot(wq_ref[...], hbt).astype(BF16)
    k = PEER_TOPK

    def select_head(h, exact):
        tie = jnp.zeros((1, qt_ref.shape[1]), F32)
        e_half = []
        for p, (rank_ref, vals_ref) in enumerate(((rank1_ref, vals1_ref), (rank2_ref, vals2_ref))):
            hp = 2 * h + p
            s = _dot(keys_ref[hp], qt_ref[hp * PEER_HALF:(hp + 1) * PEER_HALF, :])
            e_half.append(jnp.exp(s - jnp.max(s, axis=0, keepdims=True)))
            work_ref[...] = s
            tie = jnp.maximum(tie, _extract_topk(work_ref, rank_ref, vals_ref, N_KEYS, k, exact))
        v1 = vals1_ref[...]
        v2 = vals2_ref[...]
        cand_ref[...] = jnp.full(cand_ref.shape, NEG_INF, F32)
        for a, (off, nb_a) in enumerate(_CAND_ROWS):
            cand_ref[off:off + nb_a, :] = v1[a:a + 1, :] + v2[:nb_a, :]
        tie = jnp.maximum(tie, _extract_topk(cand_ref, crank_ref, cvals_ref, _N_CAND_PAD, k, exact))
        cv = cvals_ref[...]
        z = jnp.sum(jnp.exp(cv - cv[0:1, :]), axis=0, keepdims=True)
        sel = (crank_ref[...] < float(k)).astype(F32)
        tt = sel.shape[1]
        nb = None
        for off, nb_a in _CAND_ROWS:
            rows = sel[off:off + nb_a, :]
            if nb_a < k:
                rows = jnp.concatenate([rows, jnp.zeros((k - nb_a, tt), F32)], axis=0)
            nb = rows if nb is None else nb + rows
        rank2 = rank2_ref[...]
        n2 = jnp.zeros(rank2.shape, F32)
        for b in range(k):
            n2 = jnp.where(rank2 == float(b), nb[b:b + 1, :], n2)
        r1_ref[h] = rank1_ref[...]
        n2_ref[h] = n2.astype(BF16)
        e1_ref[h] = e_half[0]
        e2_ref[h] = (e_half[1] / z).astype(BF16)
        return tie

    for h in range(PEER_HEADS):
        tie = select_head(h, False)

        @pl.when(jnp.max(tie) > 0.0)
        def _():
            select_head(h, True)


def _cand_rows(k):
    rows, off = [], 0
    for a in range(k):
        nb_a = k // (a + 1)
        rows.append((off, nb_a))
        off += nb_a
    return tuple(rows), off


_CAND_ROWS, _N_CAND = _cand_rows(PEER_TOPK)
_N_CAND_PAD = -(-_N_CAND // 8) * 8


def _peer_select(x2d, norm_w, wq_bf, keys_bf, tt):
    n = x2d.shape[0]
    k = PEER_TOPK
    const2 = lambda i: (0, 0)
    tokmajor = pl.BlockSpec((PEER_HEADS, N_KEYS, tt), lambda i: (0, 0, i))
    kt = jax.ShapeDtypeStruct((PEER_HEADS, N_KEYS, n), F32)
    kt16 = jax.ShapeDtypeStruct((PEER_HEADS, N_KEYS, n), BF16)
    return pl.pallas_call(
        _peer_select_body,
        grid=(n // tt,),
        in_specs=[pl.BlockSpec((tt, D_MODEL), lambda i: (i, 0)), pl.BlockSpec((1, D_MODEL), const2),
                  pl.BlockSpec(wq_bf.shape, const2), pl.BlockSpec(keys_bf.shape, lambda i: (0, 0, 0))],
        out_specs=[pl.BlockSpec((D_MODEL, tt), lambda i: (0, i)), tokmajor, tokmajor, tokmajor, tokmajor],
        out_shape=[jax.ShapeDtypeStruct((D_MODEL, n), BF16), kt, kt16, kt, kt16],
        scratch_shapes=[pltpu.VMEM((2 * PEER_HEADS * PEER_HALF, tt), BF16),
                        pltpu.VMEM((N_KEYS, tt), F32), pltpu.VMEM((N_KEYS, tt), F32), pltpu.VMEM((N_KEYS, tt), F32),
                        pltpu.VMEM((k, tt), F32), pltpu.VMEM((k, tt), F32),
                        pltpu.VMEM((_N_CAND_PAD, tt), F32), pltpu.VMEM((_N_CAND_PAD, tt), F32),
                        pltpu.VMEM((k, tt), F32)],
        compiler_params=_cparams("parallel"),
        name="peer_select",
    )(x2d, norm_w.reshape(1, D_MODEL), wq_bf, keys_bf)


def _peer_dense_body(x_ref, xnt_ref, r1_ref, n2_ref, e1_ref, e2_ref, u_ref, vt_ref, fw_ref, o_ref,
                     yt_ref, ht_ref, pre0_ref, pre1_ref, *, blocks, nc, final_norm):
    s = pl.program_id(0)
    tt = xnt_ref.shape[1]
    reps = N_KEYS // _ROW_BCAST

    @pl.when(s == 0)
    def _():
        pre1_ref[...] = jnp.zeros(pre1_ref.shape, F32)
        yt_ref[...] = jnp.zeros(yt_ref.shape, F32)

    @pl.when(jnp.logical_and(s >= 1, (s - 1) % nc == 0))
    def _():
        yt_ref[...] = jnp.zeros(yt_ref.shape, F32)

    def bcast_row(ref, h, g):
        row = jnp.broadcast_to(ref[h, g:g + 1, :], (_ROW_BCAST, tt)).astype(BF16)
        return jnp.concatenate([row] * reps, axis=0)

    def step(pre_cur, pre_next):
        pre = pre_cur[...]
        act = (0.5 * pre * (1.0 + lax.erf(pre * (2.0 ** -0.5)))).astype(BF16)
        for g in range(blocks):
            w = None
            for h in range(PEER_HEADS):
                r1 = bcast_row(r1_ref, h, g)
                gate = bcast_row(e1_ref, h, g) * e2_ref[h]
                term = jnp.where(r1 < n2_ref[h], gate, jnp.zeros_like(gate))
                w = term if w is None else w + term
            ht_ref[g * N_KEYS:(g + 1) * N_KEYS, :] = w * act[g * N_KEYS:(g + 1) * N_KEYS, :]
        yt_ref[...] += _dot(vt_ref[...], ht_ref[...])
        pre_next[...] = _dot(u_ref[...], xnt_ref[...])

    @pl.when(s % 2 == 0)
    def _():
        step(pre1_ref, pre0_ref)

    @pl.when(s % 2 == 1)
    def _():
        step(pre0_ref, pre1_ref)

    @pl.when(jnp.logical_and(s >= 1, (s - 1) % nc == nc - 1))
    def _():
        y = x_ref[...] + yt_ref[...].T
        o_ref[...] = _rms(y, fw_ref[...]) if final_norm else y


_ROW_BCAST = 16


def _peer_dense(x2d, xnt, r1, n2, e1, e2, u_bf, vt_bf, final_w, tt, blocks, final_norm):
    n = x2d.shape[0]
    n_exp = u_bf.shape[0]
    ec = blocks * N_KEYS
    assert blocks % 8 == 0 and n_exp == N_KEYS * N_KEYS
    nc = n_exp // ec
    n_pairs = (n // tt) * nc

    def lagged(lag):
        def pair(s):
            p = jnp.clip(s - lag, 0, n_pairs - 1)
            return p // nc, p % nc
        return pair

    ahead, now = lagged(0), lagged(1)
    tok = pl.BlockSpec((tt, D_MODEL), lambda s: (now(s)[0], 0))
    rows = pl.BlockSpec((PEER_HEADS, blocks, tt), lambda s: (0, now(s)[1], now(s)[0]))
    tiles = pl.BlockSpec((PEER_HEADS, N_KEYS, tt), lambda s: (0, 0, now(s)[0]))
    return pl.pallas_call(
        functools.partial(_peer_dense_body, blocks=blocks, nc=nc, final_norm=final_norm),
        grid=(n_pairs + 1,),
        in_specs=[tok, pl.BlockSpec((D_MODEL, tt), lambda s: (0, ahead(s)[0])), rows, tiles, rows, tiles,
                  pl.BlockSpec((ec, D_MODEL), lambda s: (ahead(s)[1], 0)),
                  pl.BlockSpec((D_MODEL, ec), lambda s: (0, now(s)[1])),
                  pl.BlockSpec((1, D_MODEL), lambda s: (0, 0))],
        out_specs=tok,
        out_shape=jax.ShapeDtypeStruct((n, D_MODEL), F32),
        scratch_shapes=[pltpu.VMEM((D_MODEL, tt), F32), pltpu.VMEM((ec, tt), BF16),
                        pltpu.VMEM((ec, tt), F32), pltpu.VMEM((ec, tt), F32)],
        compiler_params=_cparams("arbitrary"),
        name="peer_dense",
    )(x2d, xnt, r1, n2, e1, e2, u_bf, vt_bf, final_w.reshape(1, D_MODEL))


def _peer(x2d, norm_w, wqt_bf, keys_bf, u_bf, vt_bf, final_w, final_norm, t_sel, t_dense, blocks):
    xnt, r1, n2, e1, e2 = _peer_select(x2d, norm_w, wqt_bf, keys_bf, t_sel)
    return _peer_dense(x2d, xnt, r1, n2, e1, e2, u_bf, vt_bf, final_w, t_dense, blocks, final_norm)


def _tile(n, want):
    return min(n, want)


def kernel(x_prompt, x_sample, cache_attn_k, cache_attn_v, cache_mem_k, cache_mem_v, state_hgrn, page_table, mem_prompt, norm_mix_w, w_in, lambda_q1, lambda_k1, lambda_q2, lambda_k2, diff_ln_w, hgrn_lower_bounds, hgrn_norm_w, w_out, norm_mem_q_w, norm_mem_kv_w, w_mq, w_mk, w_mv, w_mo, norm_ffn_w, peer_wq, peer_keys, peer_u, peer_v, final_norm_w):
    b, l = x_prompt.shape[:2]
    db, ls = x_sample.shape[:2]
    depth = w_in.shape[0]
    past = page_table.shape[1] * PAGE_SIZE
    pos_p = jnp.arange(l)
    tm_s = _tile(db * ls, 256)
    pos_s = past + (jnp.arange(tm_s) % ls)
    xp = x_prompt.reshape(b * l, D_MODEL)
    xs = x_sample.reshape(db * ls, D_MODEL)
    chunk_p = HGRN_CHUNK if l % HGRN_CHUNK == 0 else l
    chunk_s = HGRN_CHUNK if ls % HGRN_CHUNK == 0 else ls
    tq = _tile(l, 512)
    outs = [[] for _ in range(8)]
    for layer in range(depth):
        lam_init = 0.8 - 0.6 * math.exp(-0.3 * layer)
        lam = (jnp.exp(jnp.sum(lambda_q1[layer] * lambda_k1[layer]))
               - jnp.exp(jnp.sum(lambda_q2[layer] * lambda_k2[layer])) + lam_init).reshape(1, 1).astype(F32)
        last = layer == depth - 1
        w_in_bf = w_in[layer].astype(BF16)
        w_out_bf = w_out[layer].astype(BF16)
        wmq, wmk, wmv, wmo = (w[layer].astype(BF16) for w in (w_mq, w_mk, w_mv, w_mo))
        wqt_bf = peer_wq[layer].astype(BF16).T
        keys_bf = peer_keys[layer].astype(BF16).reshape(PEER_HEADS * 2, N_KEYS, PEER_HALF)
        u_bf = peer_u[layer].astype(BF16)
        vt_bf = peer_v[layer].astype(BF16).T
        peer = functools.partial(_peer, norm_w=norm_ffn_w[layer], wqt_bf=wqt_bf, keys_bf=keys_bf, u_bf=u_bf,
                                 vt_bf=vt_bf, final_w=final_norm_w, final_norm=last)

        kat, va, qat, kab, vat, qb, kb, ib, lf, gb = _inproj(
            xp, norm_mix_w[layer], w_in_bf, hgrn_lower_bounds, pos_p, layer, tq, True)
        oa = _prompt_attention(lam, qat, kab, vat, diff_ln_w[layer], 1.0 - lam_init, b, l, _tile(l, 1024))
        n_chunks = max(1, min(l, 512) // chunk_p)
        ob, sp = _hgrn(qb, kb, ib, lf, gb, jnp.zeros((b, H_B, DK_B, DV_B), F32), hgrn_norm_w[layer],
                       b, l, chunk_p, n_chunks)
        xp = _mix(xp, oa, ob, w_out_bf, _tile(l, 512))
        mk, mv = _mem_kv(mem_prompt.reshape(b * MEM_LEN, D_MODEL), norm_mem_kv_w[layer], wmk, wmv, MEM_LEN)
        tm = _tile(l, 512)
        xp = _mem_attend(xp, norm_mem_q_w[layer], wmq, mk.reshape(b, MEM_LEN, D_MODEL),
                         mv.reshape(b, MEM_LEN, D_MODEL), wmo, tm, l // tm)
        xp = peer(xp, t_sel=_tile(b * l, 512), t_dense=_tile(b * l, 512), blocks=16)
        outs[0].append(jnp.transpose(kat.reshape(b, H_A, 2, DK_A, l), (0, 4, 1, 2, 3)))
        outs[1].append(va.reshape(b, l, H_A, DV_A))
        outs[2].append(sp)
        outs[3].append(mk.reshape(b, MEM_LEN, H_M, DH_M))
        outs[4].append(mv.reshape(b, MEM_LEN, H_M, DH_M))

        ka, va, qab, kab, vab, qb, kb, ib, lf, gb = _inproj(
            xs, norm_mix_w[layer], w_in_bf, hgrn_lower_bounds, pos_s, layer, tm_s, False)
        oa = _sample_attention(lam, qab, kab, vab, cache_attn_k[layer], cache_attn_v[layer], page_table,
                               diff_ln_w[layer], 1.0 - lam_init, db, ls, min(32, page_table.shape[1]))
        ob, ss = _hgrn(qb, kb, ib, lf, gb, state_hgrn[layer], hgrn_norm_w[layer], db, ls, chunk_s, ls // chunk_s)
        xs = _mix(xs, oa, ob, w_out_bf, tm_s)
        mem_view = lambda c: jnp.transpose(c.reshape(db, MEM_LEN, H_M, 2, DH_M // 2),
                                           (0, 1, 3, 2, 4)).reshape(db, MEM_LEN * 2 * H_M, DH_M // 2)
        xs = _mem_attend(xs, norm_mem_q_w[layer], wmq, mem_view(cache_mem_k[layer]),
                         mem_view(cache_mem_v[layer]), wmo, ls, 1)
        xs = peer(xs, t_sel=_tile(db * ls, 512), t_dense=_tile(db * ls, 512), blocks=16)
        outs[5].append(ka.reshape(db, ls, H_A, 2, DK_A))
        outs[6].append(va.reshape(db, ls, H_A, DV_A))
        outs[7].append(ss)
    y_prompt = xp.reshape(b, l, D_MODEL)
    y_sample = xs.reshape(db, ls, D_MODEL)
    return (y_prompt, y_sample) + tuple(jnp.stack(o) for o in outs)
```
